```python
import jax, jax.numpy as jnp
from jax import lax
import numpy as np

D_MODEL = 1024
BATCH = 1
SEQ = 16384
DEPTH = 2

N_BRANCH = 4
BRANCH_W = 256
GLA_HEADS = 4
GLA_DK = 32
GLA_DV = 64
GLA_RANK = 16
GLA_TAU = 16.0
GLA_CHUNK = 64
POOL_WINDOWS = (2, 4, 8, 16)
POOL_GROUPS = len(POOL_WINDOWS)
POOL_GW = BRANCH_W // POOL_GROUPS
SGU_CHUNK = 128
SGU_GROUPS = 4
SGU_GW = BRANCH_W // SGU_GROUPS
MOBA_HEADS = 4
MOBA_DH = 64
MOBA_BLOCK = 256
MOBA_TOPK = 3
MOBA_QBLOCK = 128
ROPE_THETA = 10000.0
D_FF = 2816
CONV_WIDTH = 3
EPS = 1e-6

IN_SIZES = (
    GLA_HEADS * GLA_DK,
    GLA_HEADS * GLA_DK,
    GLA_HEADS * GLA_DV,
    GLA_HEADS * GLA_DV,
    GLA_RANK,
    BRANCH_W,
    2 * BRANCH_W,
    MOBA_HEADS * MOBA_DH,
    MOBA_HEADS * MOBA_DH,
    MOBA_HEADS * MOBA_DH,
)
IN_SPLITS = tuple(int(s) for s in np.cumsum(IN_SIZES)[:-1])
D_IN = int(sum(IN_SIZES))

kernel_name = 'hybrid_gated_parallel_mixers'


def rmsnorm(x, g):
    x32 = x.astype(jnp.float32)
    y = x32 * lax.rsqrt(jnp.mean(x32 * x32, axis=-1, keepdims=True) + EPS)
    return (y * g.astype(jnp.float32)).astype(x.dtype)


def rope(x, positions):
    half = x.shape[-1] // 2
    inv_freq = ROPE_THETA ** (-jnp.arange(half, dtype=jnp.float32) / half)
    ang = positions.astype(jnp.float32)[..., None] * inv_freq
    cos, sin = jnp.cos(ang)[:, :, None], jnp.sin(ang)[:, :, None]
    x32 = x.astype(jnp.float32)
    x1, x2 = x32[..., :half], x32[..., half:]
    return jnp.concatenate([x1 * cos - x2 * sin, x1 * sin + x2 * cos], axis=-1).astype(x.dtype)


def gla_mixer(q, k, v, r, a_low, w_a2, b_a, g_norm):
    bsz, seq, _ = q.shape
    n_chunk = seq // GLA_CHUNK
    f32 = jnp.float32

    def chunks(t, d):
        return jnp.moveaxis(t.astype(f32).reshape(bsz, n_chunk, GLA_CHUNK, GLA_HEADS, d), 1, 0)

    logit = a_low.astype(f32) @ w_a2.astype(f32) + b_a.astype(f32)
    log_alpha = jax.nn.log_sigmoid(logit) / GLA_TAU
    qc = chunks(q, GLA_DK) * GLA_DK ** -0.5
    kc = chunks(k, GLA_DK)
    vc = chunks(v, GLA_DV)
    bc = jnp.cumsum(chunks(log_alpha, GLA_DK), axis=2)
    causal = jnp.tril(jnp.ones((GLA_CHUNK, GLA_CHUNK), dtype=bool))[None, :, :, None, None]

    def step(state, inp):
        qn, kn, vn, bn = inp
        b_last = bn[:, -1]
        diff = bn[:, :, None] - bn[:, None, :]
        decay = jnp.exp(jnp.where(causal, diff, -jnp.inf))
        scores = jnp.einsum('bihd,bjhd,bijhd->bhij', qn, kn, decay)
        o = (jnp.einsum('bhij,bjhv->bihv', scores, vn)
             + jnp.einsum('bihd,bhdv->bihv', qn * jnp.exp(bn), state))
        k_end = kn * jnp.exp(b_last[:, None] - bn)
        new_state = jnp.exp(b_last)[..., None] * state + jnp.einsum('bjhd,bjhv->bhdv', k_end, vn)
        return new_state, o

    s0 = jnp.zeros((bsz, GLA_HEADS, GLA_DK, GLA_DV), f32)
    _, o = lax.scan(step, s0, (qc, kc, vc, bc))
    o = jnp.moveaxis(o, 0, 1).reshape(bsz, seq, GLA_HEADS, GLA_DV)
    o = o * lax.rsqrt(jnp.mean(o * o, axis=-1, keepdims=True) + EPS) * g_norm.astype(f32)
    o = o.reshape(bsz, seq, GLA_HEADS * GLA_DV) * jax.nn.silu(r.astype(f32))
    return o.astype(q.dtype)


def pool_mixer(z, w_pool, scale):
    bsz, seq, _ = z.shape
    z32 = z.astype(jnp.float32).reshape(bsz, seq, POOL_GROUPS, POOL_GW)
    csum = jnp.cumsum(z32, axis=1)
    t = jnp.arange(seq)
    means = []
    for g, w in enumerate(POOL_WINDOWS):
        c = csum[:, :, g]
        c_prev = jnp.pad(c, ((0, 0), (w, 0), (0, 0)))[:, :seq]
        count = jnp.minimum(t + 1, w).astype(jnp.float32)[None, :, None]
        means.append((c - c_prev) / count)
    mixed = jnp.stack(means, axis=2) - z32
    out = jnp.einsum('bsgc,gcd->bsgd', mixed, w_pool.astype(jnp.float32)).reshape(bsz, seq, BRANCH_W)
    return (out * scale.astype(jnp.float32)).astype(z.dtype)


def sgu_mixer(z, g_v, w_s, b_s):
    bsz, seq, _ = z.shape
    n_chunk = seq // SGU_CHUNK
    u, v = jnp.split(jax.nn.gelu(z), 2, axis=-1)
    v = rmsnorm(v, g_v).reshape(bsz, n_chunk, SGU_CHUNK, SGU_GROUPS, SGU_GW)
    w = w_s * jnp.tril(jnp.ones((SGU_CHUNK, SGU_CHUNK), dtype=w_s.dtype))
    sv = jnp.einsum('gts,bnsgc->bntgc', w, v) + jnp.transpose(b_s)[None, None, :, :, None]
    return u * sv.reshape(bsz, seq, BRANCH_W)


def moba_mixer(q, k, v, positions):
    bsz, seq, _ = q.shape
    f32 = jnp.float32
    n_blk = -(-seq // MOBA_BLOCK)
    pad = n_blk * MOBA_BLOCK - seq
    topk = min(MOBA_TOPK, n_blk)

    def heads(t):
        return t.reshape(bsz, seq, MOBA_HEADS, MOBA_DH)

    qh = jnp.transpose(rope(heads(q), positions), (0, 2, 1, 3)) * MOBA_DH ** -0.5
    kh = jnp.transpose(rope(heads(k), positions), (0, 2, 1, 3))
    vh = jnp.transpose(heads(v), (0, 2, 1, 3))

    def blocks(t):
        t = jnp.pad(t, ((0, 0), (0, 0), (0, pad), (0, 0)))
        return t.reshape(bsz, MOBA_HEADS, n_blk, MOBA_BLOCK, MOBA_DH)

    kb, vb = blocks(kh), blocks(vh)
    k_mean = jnp.mean(kb.astype(f32), axis=3).astype(kb.dtype)
    b_idx = jnp.arange(bsz)[:, None, None, None]
    h_idx = jnp.arange(MOBA_HEADS)[None, :, None, None]
    blk_ids = jnp.arange(n_blk)
    n_sel = topk * MOBA_BLOCK

    def query_block(qi):
        start = qi * MOBA_QBLOCK
        own = start // MOBA_BLOCK
        qb = lax.dynamic_slice_in_dim(qh, start, MOBA_QBLOCK, axis=2)
        gate = jnp.einsum('bhqd,bhnd->bhqn', qb, k_mean).astype(f32)
        gate = jnp.where(blk_ids < own, gate, -jnp.inf)
        _, sel = lax.top_k(gate, topk)
        valid = (jnp.arange(topk) < own)[:, None]
        k_sel = kb[b_idx, h_idx, sel]
        v_sel = vb[b_idx, h_idx, sel]
        s_sel = jnp.einsum('bhqd,bhqnkd->bhqnk', qb, k_sel).astype(f32)
        s_sel = jnp.where(valid, s_sel, -jnp.inf).reshape(bsz, MOBA_HEADS, MOBA_QBLOCK, n_sel)
        k_own = lax.dynamic_index_in_dim(kb, own, axis=2, keepdims=False)
        v_own = lax.dynamic_index_in_dim(vb, own, axis=2, keepdims=False)
        s_own = jnp.einsum('bhqd,bhkd->bhqk', qb, k_own).astype(f32)
        q_pos = start + jnp.arange(MOBA_QBLOCK)
        k_pos = own * MOBA_BLOCK + jnp.arange(MOBA_BLOCK)
        s_own = jnp.where(k_pos[None, :] <= q_pos[:, None], s_own, -jnp.inf)
        p = jax.nn.softmax(jnp.concatenate([s_sel, s_own], axis=-1), axis=-1).astype(vb.dtype)
        p_sel = p[..., :n_sel].reshape(bsz, MOBA_HEADS, MOBA_QBLOCK, topk, MOBA_BLOCK)
        p_own = p[..., n_sel:]
        return (jnp.einsum('bhqnk,bhqnkd->bhqd', p_sel, v_sel)
                + jnp.einsum('bhqk,bhkd->bhqd', p_own, v_own))

    o = lax.map(query_block, jnp.arange(seq // MOBA_QBLOCK))
    return jnp.transpose(o, (1, 0, 3, 2, 4)).reshape(bsz, seq, MOBA_HEADS * MOBA_DH)


def conv_ffn(h, w_up, conv_w, conv_b, w_down):
    up = h @ w_up
    n_ch = up.shape[-1]
    up = lax.conv_general_dilated(up, conv_w[:, None, :], window_strides=(1,),
                                  padding=[(CONV_WIDTH - 1, 0)],
                                  dimension_numbers=('NWC', 'WIO', 'NWC'),
                                  feature_group_count=n_ch) + conv_b
    a, b = jnp.split(up, 2, axis=-1)
    return (jax.nn.silu(a) * b) @ w_down


def setup_inputs(seed: int = 0) -> dict:
    key = jax.random.key(seed)
    ks = jax.random.split(key, 20)
    L, D = DEPTH, D_MODEL

    def nrm(k, shape, scale):
        return jax.random.normal(k, shape, jnp.float32) * scale

    def gain(k, shape):
        return 1.0 + 0.02 * jax.random.normal(k, shape, jnp.float32)

    return {
        'x': nrm(ks[0], (BATCH, SEQ, D), 1.0),
        'positions': jnp.broadcast_to(jnp.arange(SEQ, dtype=jnp.int32)[None], (BATCH, SEQ)),
        'g_mix': gain(ks[1], (L, D)),
        'w_in': nrm(ks[2], (L, D, D_IN), D ** -0.5),
        'w_gla_a': nrm(ks[3], (L, GLA_RANK, GLA_HEADS * GLA_DK), GLA_RANK ** -0.5),
        'b_gla_a': nrm(ks[4], (L, GLA_HEADS * GLA_DK), 0.1),
        'g_gla_norm': gain(ks[5], (L, GLA_DV)),
        'w_pool': nrm(ks[6], (L, POOL_GROUPS, POOL_GW, POOL_GW), POOL_GW ** -0.5),
        'pool_scale': gain(ks[7], (L, BRANCH_W)),
        'g_sgu': gain(ks[8], (L, BRANCH_W)),
        'w_sgu': nrm(ks[9], (L, SGU_GROUPS, SGU_CHUNK, SGU_CHUNK), SGU_CHUNK ** -0.5),
        'b_sgu': gain(ks[10], (L, SGU_GROUPS, SGU_CHUNK)),
        'w_gate': nrm(ks[11], (L, D, N_BRANCH * D), D ** -0.5),
        'w_branch': nrm(ks[12], (L, N_BRANCH, BRANCH_W, D), BRANCH_W ** -0.5),
        'w_out': nrm(ks[13], (L, D, D), D ** -0.5),
        'g_ffn': gain(ks[14], (L, D)),
        'w_up': nrm(ks[15], (L, D, 2 * D_FF), D ** -0.5),
        'conv_w': nrm(ks[16], (L, CONV_WIDTH, 2 * D_FF), CONV_WIDTH ** -0.5),
        'conv_b': nrm(ks[17], (L, 2 * D_FF), 0.02),
        'w_down': nrm(ks[18], (L, D_FF, D), D_FF ** -0.5),
        'g_final': gain(ks[19], (D,)),
    }


def reference(x, positions, g_mix, w_in, w_gla_a, b_gla_a, g_gla_norm, w_pool, pool_scale,
              g_sgu, w_sgu, b_sgu, w_gate, w_branch, w_out, g_ffn, w_up, conv_w, conv_b,
              w_down, g_final):
    D = D_MODEL
    for l in range(DEPTH):
        h = rmsnorm(x, g_mix[l])
        proj = h @ w_in[l]
        (gq, gk, gv, gr, ga, pz, sz, mq, mk, mv) = jnp.split(proj, IN_SPLITS, axis=-1)
        o_gla = gla_mixer(gq, gk, gv, gr, ga, w_gla_a[l], b_gla_a[l], g_gla_norm[l])
        o_pool = pool_mixer(pz, w_pool[l], pool_scale[l])
        o_sgu = sgu_mixer(sz, g_sgu[l], w_sgu[l], b_sgu[l])
        o_moba = moba_mixer(mq, mk, mv, positions)
        gates = jax.nn.sigmoid(h @ w_gate[l])
        branches = (o_gla, o_pool, o_sgu, o_moba)
        merged = gates[..., :D] * (branches[0] @ w_branch[l, 0])
        for i in range(1, N_BRANCH):
            merged = merged + gates[..., i * D:(i + 1) * D] * (branches[i] @ w_branch[l, i])
        x = x + merged @ w_out[l]
        h = rmsnorm(x, g_ffn[l])
        x = x + conv_ffn(h, w_up[l], conv_w[l], conv_b[l], w_down[l])
    return rmsnorm(x, g_final)
```

```python
import functools

import numpy as np
import jax
import jax.numpy as jnp
from jax import lax
from jax.experimental import pallas as pl
from jax.experimental.pallas import tpu as pltpu

F32 = jnp.float32
BF16 = jnp.bfloat16
HIGHEST = lax.Precision.HIGHEST

D_MODEL = 1024
N_BRANCH = 4
BRANCH_W = 256
GLA_HEADS, GLA_DK, GLA_DV, GLA_RANK, GLA_TAU = 4, 32, 64, 16, 16.0
GLA_QK = GLA_HEADS * GLA_DK
GLA_V = GLA_HEADS * GLA_DV
POOL_WINDOWS = (2, 4, 8, 16)
POOL_GW = BRANCH_W // len(POOL_WINDOWS)
SGU_CHUNK, SGU_GROUPS = 128, 4
SGU_GW = BRANCH_W // SGU_GROUPS
MOBA_HEADS, MOBA_DH, MOBA_BLOCK, MOBA_TOPK = 4, 64, 256, 3
MOBA_W = MOBA_HEADS * MOBA_DH
ROPE_THETA = 10000.0
D_FF = 2816
EPS = 1e-6

LANES = 128
VMEM_LIMIT_BYTES = 56 * 1024 * 1024

ROW_TILE = 512
GLA_TILE = 256
GLA_SUB = 16
HALO = 16
FF_CHUNK = 256
NEG_BIG = -1e30

C_GLA = 0
C_GA = C_GLA + 2 * GLA_QK + 2 * GLA_V
C_LOC = C_GA + LANES
C_MQ = C_LOC + BRANCH_W + 2 * BRANCH_W
C_MK = C_MQ + MOBA_W
C_MV = C_MK + MOBA_W
C_END = C_MV + MOBA_W
GLA_IN_W = C_GA + GLA_QK
LOC_W = 3 * BRANCH_W


def _rms(x, g):
    return x * lax.rsqrt(jnp.mean(x * x, axis=-1, keepdims=True) + EPS) * g


def _sigmoid(x):
    return 1.0 / (1.0 + jnp.exp(-x))


def _params(*sem):
    return pltpu.CompilerParams(dimension_semantics=sem, vmem_limit_bytes=VMEM_LIMIT_BYTES)


def _const_spec(shape):
    return pl.BlockSpec(shape, lambda i: (0,) * len(shape))


def _rope_tile(x, cos, sin, first_half):
    partner = jnp.where(first_half, -pltpu.roll(x, LANES - MOBA_DH // 2, axis=1),
                        pltpu.roll(x, MOBA_DH // 2, axis=1))
    return x * cos + partner * sin


def _inproj_kernel(x_ref, pos_ref, g_ref, w_ref, wa2_ref, ba_ref, invf_ref,
                   gla_ref, loc_ref, qt_ref, k_ref, vt_ref, kmean_ref):
    rows = x_ref.shape[0]
    h = _rms(x_ref[...], g_ref[...])
    proj = jnp.dot(h.astype(BF16), w_ref[...], preferred_element_type=F32)

    gla_ref[:, 0:C_GA] = proj[:, C_GLA:C_GA]
    logit = jnp.dot(proj[:, C_GA:C_LOC], wa2_ref[...], precision=HIGHEST,
                    preferred_element_type=F32) + ba_ref[...]
    log_sig = jnp.minimum(logit, 0.0) - jnp.log(1.0 + jnp.exp(-jnp.abs(logit)))
    gla_ref[:, C_GA:GLA_IN_W] = log_sig * (1.0 / GLA_TAU)

    loc_ref[...] = proj[:, C_LOC:C_MQ]

    ang = pos_ref[...].astype(F32) * invf_ref[...]
    cos, sin = jnp.cos(ang), jnp.sin(ang)
    lane = lax.broadcasted_iota(jnp.int32, (rows, LANES), 1)
    first_half = (lane % MOBA_DH) < (MOBA_DH // 2)

    def roped(c0):
        return jnp.concatenate(
            [_rope_tile(proj[:, c0 + t * LANES:c0 + (t + 1) * LANES], cos, sin, first_half)
             for t in range(MOBA_W // LANES)], axis=1)

    q = roped(C_MQ) * (MOBA_DH ** -0.5)
    k = roped(C_MK)
    v = proj[:, C_MV:C_END]
    k_ref[...] = k.astype(BF16)
    for r in range(rows // MOBA_BLOCK):
        sl = slice(r * MOBA_BLOCK, (r + 1) * MOBA_BLOCK)
        qt_ref[r] = q[sl].T
        vt_ref[r] = v[sl].T.astype(BF16)
        kmean_ref[0, r:r + 1, :] = jnp.mean(k[sl], axis=0, keepdims=True)


def _inproj(x, pos, g, w_cat, wa2, ba, invf):
    seq = x.shape[0]
    nblk = ROW_TILE // MOBA_BLOCK
    row = lambda w: pl.BlockSpec((ROW_TILE, w), lambda i: (i, 0))
    blk3 = pl.BlockSpec((nblk, MOBA_W, MOBA_BLOCK), lambda i: (i, 0, 0))
    return pl.pallas_call(
        _inproj_kernel,
        grid=(seq // ROW_TILE,),
        in_specs=[row(D_MODEL), row(1), _const_spec((1, D_MODEL)), _const_spec((D_MODEL, C_END)),
                  _const_spec((LANES, GLA_QK)), _const_spec((1, GLA_QK)), _const_spec((1, LANES))],
        out_specs=[row(GLA_IN_W), row(LOC_W), blk3, row(MOBA_W), blk3,
                   pl.BlockSpec((1, nblk, MOBA_W), lambda i: (i, 0, 0))],
        out_shape=[jax.ShapeDtypeStruct((seq, GLA_IN_W), F32),
                   jax.ShapeDtypeStruct((seq, LOC_W), F32),
                   jax.ShapeDtypeStruct((seq // MOBA_BLOCK, MOBA_W, MOBA_BLOCK), F32),
                   jax.ShapeDtypeStruct((seq, MOBA_W), BF16),
                   jax.ShapeDtypeStruct((seq // MOBA_BLOCK, MOBA_W, MOBA_BLOCK), BF16),
                   jax.ShapeDtypeStruct((seq // ROW_TILE, nblk, MOBA_W), F32)],
        compiler_params=_params("parallel"),
        name="inproj",
    )(x, pos, g, w_cat, wa2, ba, invf)


def _gla_kernel(gin_ref, gnorm_ref, out_ref, state_ref, kpad_ref, vpad_ref, bpad_ref,
                qb_ref, ke_ref, eb_ref, o_ref):
    rows = gin_ref.shape[0]

    @pl.when(pl.program_id(0) == 0)
    def _():
        state_ref[...] = jnp.zeros_like(state_ref)
        kpad_ref[0:HALO, :] = jnp.zeros((HALO, GLA_QK), F32)
        vpad_ref[0:HALO, :] = jnp.zeros((HALO, GLA_V), F32)
        bpad_ref[0:HALO, :] = jnp.zeros((HALO, GLA_QK), F32)

    q = gin_ref[:, 0:GLA_QK] * (GLA_DK ** -0.5)
    k = gin_ref[:, GLA_QK:2 * GLA_QK]
    v = gin_ref[:, 2 * GLA_QK:2 * GLA_QK + GLA_V]
    la = gin_ref[:, C_GA:GLA_IN_W]

    rowmod = lax.broadcasted_iota(jnp.int32, (rows, GLA_QK), 0) % GLA_SUB
    b = la
    shift = 1
    while shift < GLA_SUB:
        b = b + jnp.where(rowmod >= shift, pltpu.roll(b, shift, axis=0), 0.0)
        shift *= 2
    b_last = jnp.broadcast_to(b.reshape(rows // GLA_SUB, GLA_SUB, GLA_QK)[:, GLA_SUB - 1:GLA_SUB, :],
                              (rows // GLA_SUB, GLA_SUB, GLA_QK)).reshape(rows, GLA_QK)
    eb = jnp.exp(b)
    qb_ref[...] = (q * eb).astype(BF16)
    ke_ref[...] = (k * jnp.exp(b_last - b)).astype(BF16)
    eb_ref[...] = eb

    kpad_ref[HALO:HALO + rows, :] = k
    vpad_ref[HALO:HALO + rows, :] = v
    bpad_ref[HALO:HALO + rows, :] = b

    head_sum = (lax.broadcasted_iota(jnp.int32, (GLA_QK, GLA_V), 0) // GLA_DK ==
                lax.broadcasted_iota(jnp.int32, (GLA_QK, GLA_V), 1) // GLA_DV).astype(BF16)
    o = jnp.zeros((rows, GLA_V), F32)
    for d in range(GLA_SUB):
        lo = HALO - d
        kd = kpad_ref[lo:lo + rows, :]
        bd = bpad_ref[lo:lo + rows, :]
        vd = vpad_ref[lo:lo + rows, :]
        decay = jnp.exp(jnp.where(rowmod >= d, b - bd, NEG_BIG))
        p = (q * kd * decay).astype(BF16)
        o = o + jnp.dot(p, head_sum, preferred_element_type=F32) * vd
    o_ref[...] = o

    state_mask = (lax.broadcasted_iota(jnp.int32, (GLA_V, GLA_QK), 0) // GLA_DV ==
                  lax.broadcasted_iota(jnp.int32, (GLA_V, GLA_QK), 1) // GLA_DK)

    def step(s, state):
        r0 = pl.multiple_of(s * GLA_SUB, GLA_SUB)
        blk = pl.ds(r0, GLA_SUB)
        o_ref[blk, :] += lax.dot_general(qb_ref[blk, :], state.astype(BF16),
                                         (((1,), (1,)), ((), ())), preferred_element_type=F32)
        v_b = gin_ref[blk, 2 * GLA_QK:2 * GLA_QK + GLA_V].astype(BF16)
        kv = lax.dot_general(v_b, ke_ref[blk, :], (((0,), (0,)), ((), ())),
                             preferred_element_type=F32)
        decay_row = eb_ref[pl.ds(r0 + GLA_SUB - 1, 1), :]
        return state * decay_row + jnp.where(state_mask, kv, 0.0)

    state_ref[...] = lax.fori_loop(0, rows // GLA_SUB, step, state_ref[...])

    o = o_ref[...]
    head_mean = (lax.broadcasted_iota(jnp.int32, (GLA_V, GLA_V), 0) // GLA_DV ==
                 lax.broadcasted_iota(jnp.int32, (GLA_V, GLA_V), 1) // GLA_DV).astype(F32) * (1.0 / GLA_DV)
    ms = jnp.dot(o * o, head_mean, precision=HIGHEST, preferred_element_type=F32)
    r = gin_ref[:, 2 * GLA_QK + GLA_V:C_GA]
    out_ref[...] = o * lax.rsqrt(ms + EPS) * gnorm_ref[...] * (r * _sigmoid(r))


def _gla(gla_in, gnorm):
    seq = gla_in.shape[0]
    return pl.pallas_call(
        _gla_kernel,
        grid=(seq // GLA_TILE,),
        in_specs=[pl.BlockSpec((GLA_TILE, GLA_IN_W), lambda i: (i, 0)), _const_spec((1, GLA_V))],
        out_specs=pl.BlockSpec((GLA_TILE, GLA_V), lambda i: (i, 0)),
        out_shape=jax.ShapeDtypeStruct((seq, GLA_V), F32),
        scratch_shapes=[pltpu.VMEM((GLA_V, GLA_QK), F32),
                        pltpu.VMEM((HALO + GLA_TILE, GLA_QK), F32),
                        pltpu.VMEM((HALO + GLA_TILE, GLA_V), F32),
                        pltpu.VMEM((HALO + GLA_TILE, GLA_QK), F32),
                        pltpu.VMEM((GLA_TILE, GLA_QK), BF16),
                        pltpu.VMEM((GLA_TILE, GLA_QK), BF16),
                        pltpu.VMEM((GLA_TILE, GLA_QK), F32),
                        pltpu.VMEM((GLA_TILE, GLA_V), F32)],
        compiler_params=_params("arbitrary"),
        name="gla",
    )(gla_in, gnorm)


def _local_kernel(loc_ref, halo_ref, wpool_ref, pscale_ref, gsgu_ref, wsgu_ref, bsgu_ref,
                  pool_ref, sgu_ref, zpad_ref):
    rows = loc_ref.shape[0]
    i = pl.program_id(0)

    z = loc_ref[:, 0:BRANCH_W]
    zpad_ref[0:HALO, :] = jnp.where(i > 0, halo_ref[...], 0.0)
    zpad_ref[HALO:HALO + rows, :] = z
    t = i * rows + lax.broadcasted_iota(jnp.int32, (rows, BRANCH_W), 0)
    group = lax.broadcasted_iota(jnp.int32, (rows, BRANCH_W), 1) // POOL_GW
    acc = z
    mean = jnp.zeros_like(z)
    width = 1
    for g, w in enumerate(POOL_WINDOWS):
        for back in range(width, w):
            acc = acc + zpad_ref[HALO - back:HALO - back + rows, :]
        width = w
        count = jnp.minimum(t + 1, w).astype(F32)
        mean = jnp.where(group == g, acc / count, mean)
    mixed = (mean - z).astype(BF16)
    pool_ref[...] = jnp.dot(mixed, wpool_ref[...], preferred_element_type=F32) * pscale_ref[...]

    zs = loc_ref[:, BRANCH_W:LOC_W]
    gz = 0.5 * zs * (1.0 + jnp.tanh(np.sqrt(2.0 / np.pi).astype(np.float32) * (zs + 0.044715 * (zs * zs * zs))))
    u = gz[:, 0:BRANCH_W]
    vn = _rms(gz[:, BRANCH_W:], gsgu_ref[...]).astype(BF16)
    tril = (lax.broadcasted_iota(jnp.int32, (SGU_CHUNK, SGU_CHUNK), 0) >=
            lax.broadcasted_iota(jnp.int32, (SGU_CHUNK, SGU_CHUNK), 1))
    w_low = [jnp.where(tril, wsgu_ref[g], 0.0).astype(BF16) for g in range(SGU_GROUPS)]
    cgroup = lax.broadcasted_iota(jnp.int32, (SGU_CHUNK, BRANCH_W), 1) // SGU_GW
    for c in range(rows // SGU_CHUNK):
        sl = slice(c * SGU_CHUNK, (c + 1) * SGU_CHUNK)
        sv = bsgu_ref[...]
        for g in range(SGU_GROUPS):
            mix = jnp.dot(w_low[g], vn[sl], preferred_element_type=F32)
            sv = sv + jnp.where(cgroup == g, mix, 0.0)
        sgu_ref[sl, :] = u[sl] * sv


def _local(loc, wpool_bd, pscale, gsgu, wsgu, bsgu_full):
    seq = loc.shape[0]
    per = ROW_TILE // HALO
    return pl.pallas_call(
        _local_kernel,
        grid=(seq // ROW_TILE,),
        in_specs=[pl.BlockSpec((ROW_TILE, LOC_W), lambda i: (i, 0)),
                  pl.BlockSpec((HALO, BRANCH_W), lambda i: (jnp.maximum(i * per - 1, 0), 0)),
                  _const_spec((BRANCH_W, BRANCH_W)), _const_spec((1, BRANCH_W)), _const_spec((1, BRANCH_W)),
                  _const_spec((SGU_GROUPS, SGU_CHUNK, SGU_CHUNK)), _const_spec((SGU_CHUNK, BRANCH_W))],
        out_specs=[pl.BlockSpec((ROW_TILE, BRANCH_W), lambda i: (i, 0))] * 2,
        out_shape=[jax.ShapeDtypeStruct((seq, BRANCH_W), F32)] * 2,
        scratch_shapes=[pltpu.VMEM((HALO + ROW_TILE, BRANCH_W), F32)],
        compiler_params=_params("parallel"),
        name="local_mixers",
    )(loc, loc, wpool_bd, pscale, gsgu, wsgu, bsgu_full)


def _moba_kernel(qt_ref, k_ref, vt_ref, kmean_ref, o_ref, sel_ref, acc_ref):
    own = pl.program_id(0)
    nblk = k_ref.shape[0]
    blk_id = lax.broadcasted_iota(jnp.int32, (nblk, MOBA_BLOCK), 0)
    pair_row = lax.broadcasted_iota(jnp.int32, (LANES, MOBA_BLOCK), 0) // MOBA_DH
    key_pos = lax.broadcasted_iota(jnp.int32, (MOBA_BLOCK, MOBA_BLOCK), 0)
    q_pos = lax.broadcasted_iota(jnp.int32, (MOBA_BLOCK, MOBA_BLOCK), 1)

    for h in range(MOBA_HEADS):
        lanes = pl.ds((h // 2) * LANES, LANES)
        qt = jnp.where(pair_row == (h % 2), qt_ref[0, (h // 2) * LANES:(h // 2 + 1) * LANES, :], 0.0)

        gate = jnp.dot(kmean_ref[:, lanes], qt, precision=HIGHEST, preferred_element_type=F32)
        gate = jnp.where(blk_id < own, gate, -jnp.inf)
        sel = jnp.zeros((nblk, MOBA_BLOCK), F32)
        for _ in range(MOBA_TOPK):
            best = jnp.max(gate, axis=0, keepdims=True)
            first = jnp.min(jnp.where(gate == best, blk_id, nblk), axis=0, keepdims=True)
            pick = (blk_id == first) & (best > -jnp.inf)
            sel = jnp.where(pick, 1.0, sel)
            gate = jnp.where(pick, -jnp.inf, gate)
        sel_ref[...] = sel

        qt_b = qt.astype(BF16)
        vrows = pl.ds(h * MOBA_DH, MOBA_DH)

        def attend(j, carry, selected, causal):
            m_run, l_run, acc = carry
            s = jnp.dot(k_ref[j, :, lanes], qt_b, preferred_element_type=F32)
            if causal:
                s = jnp.where(key_pos <= q_pos, s, NEG_BIG)
            blk_max = jnp.max(s, axis=0, keepdims=True)
            m_new = jnp.maximum(m_run, jnp.where(selected, blk_max, NEG_BIG))
            p = jnp.exp(s - jnp.where(selected, m_new, -NEG_BIG))
            alpha = jnp.exp(m_run - m_new)
            l_new = l_run * alpha + jnp.sum(p, axis=0, keepdims=True)
            pv = jnp.dot(vt_ref[j, vrows, :], p.astype(BF16), preferred_element_type=F32)
            return m_new, l_new, acc * alpha + pv

        def past(j, carry):
            return attend(j, carry, sel_ref[pl.ds(j, 1), :] > 0.0, False)

        init = (jnp.full((1, MOBA_BLOCK), NEG_BIG, F32), jnp.zeros((1, MOBA_BLOCK), F32),
                jnp.zeros((MOBA_DH, MOBA_BLOCK), F32))
        carry = lax.fori_loop(0, own, past, init)
        _, l_fin, acc = attend(own, carry, jnp.full((1, MOBA_BLOCK), True), True)
        acc_ref[vrows, :] = acc / l_fin

    o_ref[...] = acc_ref[...].T


def _moba(qt3, k3, vt3, kmean):
    nblk = k3.shape[0]
    seq = nblk * MOBA_BLOCK
    return pl.pallas_call(
        _moba_kernel,
        grid=(nblk,),
        in_specs=[pl.BlockSpec((1, MOBA_W, MOBA_BLOCK), lambda i: (i, 0, 0)),
                  _const_spec((nblk, MOBA_BLOCK, MOBA_W)), _const_spec((nblk, MOBA_W, MOBA_BLOCK)),
                  _const_spec((nblk, MOBA_W))],
        out_specs=pl.BlockSpec((MOBA_BLOCK, MOBA_W), lambda i: (i, 0)),
        out_shape=jax.ShapeDtypeStruct((seq, MOBA_W), F32),
        scratch_shapes=[pltpu.VMEM((nblk, MOBA_BLOCK), F32), pltpu.VMEM((MOBA_W, MOBA_BLOCK), F32)],
        compiler_params=_params("parallel"),
        name="moba",
    )(qt3, k3, vt3, kmean)


def _merge_kernel(x_ref, g_ref, gla_ref, pool_ref, sgu_ref, moba_ref, wgate_ref, wbr_ref, wout_ref, y_ref):
    x = x_ref[...]
    h = _rms(x, g_ref[...]).astype(BF16)
    merged = None
    for b, br_ref in enumerate((gla_ref, pool_ref, sgu_ref, moba_ref)):
        gate = _sigmoid(jnp.dot(h, wgate_ref[:, b * D_MODEL:(b + 1) * D_MODEL], preferred_element_type=F32))
        term = gate * jnp.dot(br_ref[...].astype(BF16), wbr_ref[b], preferred_element_type=F32)
        merged = term if merged is None else merged + term
    y_ref[...] = x + jnp.dot(merged.astype(BF16), wout_ref[...], preferred_element_type=F32)


def _merge(x, g, branches, wgate, wbr, wout):
    seq = x.shape[0]
    row = lambda w: pl.BlockSpec((ROW_TILE, w), lambda i: (i, 0))
    return pl.pallas_call(
        _merge_kernel,
        grid=(seq // ROW_TILE,),
        in_specs=[row(D_MODEL), _const_spec((1, D_MODEL))] + [row(BRANCH_W)] * N_BRANCH +
                 [_const_spec((D_MODEL, N_BRANCH * D_MODEL)), _const_spec((N_BRANCH, BRANCH_W, D_MODEL)),
                  _const_spec((D_MODEL, D_MODEL))],
        out_specs=row(D_MODEL),
        out_shape=jax.ShapeDtypeStruct((seq, D_MODEL), F32),
        compiler_params=_params("parallel"),
        name="merge",
    )(x, g, *branches, wgate, wbr, wout)


def _ffn_kernel(x_ref, halo_ref, g_ref, wup_ref, cw_ref, cb_ref, wdown_ref, gfin_ref, y_ref,
                h_ref, acc_ref, *, final_norm):
    rows = x_ref.shape[0]
    i = pl.program_id(0)
    x = x_ref[...]
    h_ref[0:HALO, :] = _rms(halo_ref[...], g_ref[...]).astype(BF16)
    h_ref[HALO:HALO + rows, :] = _rms(x, g_ref[...]).astype(BF16)
    padded_row = lax.broadcasted_iota(jnp.int32, (HALO + rows, FF_CHUNK), 0)
    before_start = (padded_row < HALO) & (i == 0)

    def conv(col0):
        up = jnp.dot(h_ref[...], wup_ref[:, pl.ds(col0, FF_CHUNK)], preferred_element_type=F32)
        up = jnp.where(before_start, 0.0, up)
        cw = cw_ref[:, pl.ds(col0, FF_CHUNK)]
        out = (pltpu.roll(up, 2, axis=0) * cw[0:1] + pltpu.roll(up, 1, axis=0) * cw[1:2] + up * cw[2:3]
               + cb_ref[:, pl.ds(col0, FF_CHUNK)])
        return out[HALO:]

    acc_ref[...] = x
    for c in range(D_FF // FF_CHUNK):
        a = conv(c * FF_CHUNK)
        b = conv(D_FF + c * FF_CHUNK)
        act = (a * _sigmoid(a) * b).astype(BF16)
        acc_ref[...] += jnp.dot(act, wdown_ref[c * FF_CHUNK:(c + 1) * FF_CHUNK, :], preferred_element_type=F32)
    y = acc_ref[...]
    y_ref[...] = _rms(y, gfin_ref[...]) if final_norm else y


def _ffn(x, g, wup, cw, cb, wdown, gfin, final_norm):
    seq = x.shape[0]
    per = ROW_TILE // HALO
    row = pl.BlockSpec((ROW_TILE, D_MODEL), lambda i: (i, 0))
    return pl.pallas_call(
        functools.partial(_ffn_kernel, final_norm=final_norm),
        grid=(seq // ROW_TILE,),
        in_specs=[row, pl.BlockSpec((HALO, D_MODEL), lambda i: (jnp.maximum(i * per - 1, 0), 0)),
                  _const_spec((1, D_MODEL)), _const_spec((D_MODEL, 2 * D_FF)), _const_spec((3, 2 * D_FF)),
                  _const_spec((1, 2 * D_FF)), _const_spec((D_FF, D_MODEL)), _const_spec((1, D_MODEL))],
        out_specs=row,
        out_shape=jax.ShapeDtypeStruct((seq, D_MODEL), F32),
        scratch_shapes=[pltpu.VMEM((HALO + ROW_TILE, D_MODEL), BF16), pltpu.VMEM((ROW_TILE, D_MODEL), F32)],
        compiler_params=_params("parallel"),
        name="conv_ffn",
    )(x, x, g, wup, cw, cb, wdown, gfin)


def _pack_w_in(w_in):
    n_front = 2 * GLA_QK + 2 * GLA_V
    pad = jnp.zeros((w_in.shape[0], LANES - GLA_RANK), w_in.dtype)
    return jnp.concatenate([w_in[:, :n_front + GLA_RANK], pad, w_in[:, n_front + GLA_RANK:]], axis=1).astype(BF16)


def _block_diag(w):
    g, a, b = w.shape
    eye = jnp.eye(g, dtype=w.dtype)
    return (eye[:, None, :, None] * w[:, :, None, :]).reshape(g * a, g * b)


def kernel(x, positions, g_mix, w_in, w_gla_a, b_gla_a, g_gla_norm, w_pool, pool_scale, g_sgu, w_sgu, b_sgu,
           w_gate, w_branch, w_out, g_ffn, w_up, conv_w, conv_b, w_down, g_final):
    bsz, seq, d = x.shape
    assert bsz == 1 and d == D_MODEL and seq % ROW_TILE == 0 and seq % GLA_TILE == 0
    depth = w_in.shape[0]
    xs = x[0]
    pos = positions.reshape(seq, 1)
    half = MOBA_DH // 2
    inv_freq = ROPE_THETA ** (-jnp.arange(half, dtype=F32) / half)
    invf = jnp.tile(inv_freq, LANES // half)[None, :]
    row = lambda a: a.reshape(1, -1)

    for l in range(depth):
        wa2 = jnp.concatenate([w_gla_a[l], jnp.zeros((LANES - GLA_RANK, GLA_QK), F32)], axis=0)
        gla_in, loc, qt3, k, vt3, kmean = _inproj(xs, pos, row(g_mix[l]), _pack_w_in(w_in[l]), wa2,
                                                  row(b_gla_a[l]), invf)
        o_gla = _gla(gla_in, row(jnp.tile(g_gla_norm[l], GLA_HEADS)))
        o_pool, o_sgu = _local(loc, _block_diag(w_pool[l]).astype(BF16), row(pool_scale[l]), row(g_sgu[l]),
                               w_sgu[l], jnp.repeat(b_sgu[l].T, SGU_GW, axis=1))
        o_moba = _moba(qt3, k.reshape(seq // MOBA_BLOCK, MOBA_BLOCK, MOBA_W), vt3,
                       kmean.reshape(seq // MOBA_BLOCK, MOBA_W))
        xs = _merge(xs, row(g_mix[l]), (o_gla, o_pool, o_sgu, o_moba), w_gate[l].astype(BF16),
                    w_branch[l].astype(BF16), w_out[l].astype(BF16))
        xs = _ffn(xs, row(g_ffn[l]), w_up[l].astype(BF16), conv_w[l], row(conv_b[l]), w_down[l].astype(BF16),
                  row(g_final), final_norm=(l == depth - 1))
    return xs[None]
```

```python
import functools

import numpy as np
import jax
import jax.numpy as jnp
from jax import lax
from jax.experimental import pallas as pl
from jax.experimental.pallas import tpu as pltpu

F32 = jnp.float32
BF16 = jnp.bfloat16
HIGHEST = lax.Precision.HIGHEST

D_MODEL = 1024
N_BRANCH = 4
BRANCH_W = 256
GLA_HEADS, GLA_DK, GLA_DV, GLA_RANK, GLA_TAU = 4, 32, 64, 16, 16.0
GLA_QK = GLA_HEADS * GLA_DK
GLA_V = GLA_HEADS * GLA_DV
POOL_WINDOWS = (2, 4, 8, 16)
POOL_GW = BRANCH_W // len(POOL_WINDOWS)
SGU_CHUNK, SGU_GROUPS = 128, 4
SGU_GW = BRANCH_W // SGU_GROUPS
MOBA_HEADS, MOBA_DH, MOBA_BLOCK, MOBA_TOPK = 4, 64, 256, 3
MOBA_W = MOBA_HEADS * MOBA_DH
MOBA_PV_ROWS = MOBA_DH + 16
ROPE_THETA = 10000.0
D_FF = 2816
EPS = 1e-6

LANES = 128
VMEM_LIMIT_BYTES = 56 * 1024 * 1024

ROW_TILE = 512
GLA_TILE = 256
GLA_SUB = 16
HALO = 16
FF_CHUNK = 256
NEG_BIG = -1e30

C_GLA = 0
C_GA = C_GLA + 2 * GLA_QK + 2 * GLA_V
C_LOC = C_GA + LANES
C_MQ = C_LOC + BRANCH_W + 2 * BRANCH_W
C_MK = C_MQ + MOBA_W
C_MV = C_MK + MOBA_W
C_END = C_MV + MOBA_W
GLA_IN_W = C_GA + GLA_QK
LOC_W = 3 * BRANCH_W


def _rms(x, g):
    return x * lax.rsqrt(jnp.mean(x * x, axis=-1, keepdims=True) + EPS) * g


def _sigmoid(x):
    return 1.0 / (1.0 + jnp.exp(-x))


def _params(*sem):
    return pltpu.CompilerParams(dimension_semantics=sem, vmem_limit_bytes=VMEM_LIMIT_BYTES)


def _const_spec(shape):
    return pl.BlockSpec(shape, lambda i: (0,) * len(shape))


def _rope_tile(x, cos, sin, first_half):
    partner = jnp.where(first_half, -pltpu.roll(x, LANES - MOBA_DH // 2, axis=1),
                        pltpu.roll(x, MOBA_DH // 2, axis=1))
    return x * cos + partner * sin


def _inproj_kernel(x_ref, pos_ref, g_ref, w_ref, wa2_ref, ba_ref, invf_ref,
                   gla_ref, loc_ref, qt_ref, k_ref, vt_ref, kmean_ref):
    rows = x_ref.shape[0]
    h = _rms(x_ref[...], g_ref[...])
    proj = jnp.dot(h.astype(BF16), w_ref[...], preferred_element_type=F32)

    gla_ref[:, 0:C_GA] = proj[:, C_GLA:C_GA]
    logit = jnp.dot(proj[:, C_GA:C_LOC], wa2_ref[...], precision=HIGHEST,
                    preferred_element_type=F32) + ba_ref[...]
    log_sig = jnp.minimum(logit, 0.0) - jnp.log(1.0 + jnp.exp(-jnp.abs(logit)))
    gla_ref[:, C_GA:GLA_IN_W] = log_sig * (1.0 / GLA_TAU)

    loc_ref[...] = proj[:, C_LOC:C_MQ]

    ang = pos_ref[...].astype(F32) * invf_ref[...]
    cos, sin = jnp.cos(ang), jnp.sin(ang)
    lane = lax.broadcasted_iota(jnp.int32, (rows, LANES), 1)
    first_half = (lane % MOBA_DH) < (MOBA_DH // 2)

    def roped(c0):
        return jnp.concatenate(
            [_rope_tile(proj[:, c0 + t * LANES:c0 + (t + 1) * LANES], cos, sin, first_half)
             for t in range(MOBA_W // LANES)], axis=1)

    q = roped(C_MQ) * (MOBA_DH ** -0.5)
    k = roped(C_MK)
    v = proj[:, C_MV:C_END]
    k_ref[...] = k.astype(BF16)
    for r in range(rows // MOBA_BLOCK):
        sl = slice(r * MOBA_BLOCK, (r + 1) * MOBA_BLOCK)
        qt_ref[r] = q[sl].T
        vt_ref[r] = v[sl].T.astype(BF16)
        kmean_ref[0, r:r + 1, :] = jnp.mean(k[sl], axis=0, keepdims=True)


def _inproj(x, pos, g, w_cat, wa2, ba, invf):
    seq = x.shape[0]
    nblk = ROW_TILE // MOBA_BLOCK
    row = lambda w: pl.BlockSpec((ROW_TILE, w), lambda i: (i, 0))
    blk3 = pl.BlockSpec((nblk, MOBA_W, MOBA_BLOCK), lambda i: (i, 0, 0))
    return pl.pallas_call(
        _inproj_kernel,
        grid=(seq // ROW_TILE,),
        in_specs=[row(D_MODEL), row(1), _const_spec((1, D_MODEL)), _const_spec((D_MODEL, C_END)),
                  _const_spec((LANES, GLA_QK)), _const_spec((1, GLA_QK)), _const_spec((1, LANES))],
        out_specs=[row(GLA_IN_W), row(LOC_W), blk3, row(MOBA_W), blk3,
                   pl.BlockSpec((1, nblk, MOBA_W), lambda i: (i, 0, 0))],
        out_shape=[jax.ShapeDtypeStruct((seq, GLA_IN_W), F32),
                   jax.ShapeDtypeStruct((seq, LOC_W), F32),
                   jax.ShapeDtypeStruct((seq // MOBA_BLOCK, MOBA_W, MOBA_BLOCK), F32),
                   jax.ShapeDtypeStruct((seq, MOBA_W), BF16),
                   jax.ShapeDtypeStruct((seq // MOBA_BLOCK, MOBA_W, MOBA_BLOCK), BF16),
                   jax.ShapeDtypeStruct((seq // ROW_TILE, nblk, MOBA_W), F32)],
        compiler_params=_params("parallel"),
        name="inproj",
    )(x, pos, g, w_cat, wa2, ba, invf)


def _gla_kernel(gin_ref, gnorm_ref, out_ref, state_ref, kpad_ref, vpad_ref, bpad_ref,
                qb_ref, ke_ref, eb_ref, o_ref):
    rows = gin_ref.shape[0]

    @pl.when(pl.program_id(0) == 0)
    def _():
        state_ref[...] = jnp.zeros_like(state_ref)
        kpad_ref[0:HALO, :] = jnp.zeros((HALO, GLA_QK), F32)
        vpad_ref[0:HALO, :] = jnp.zeros((HALO, GLA_V), F32)
        bpad_ref[0:HALO, :] = jnp.zeros((HALO, GLA_QK), F32)

    q = gin_ref[:, 0:GLA_QK] * (GLA_DK ** -0.5)
    k = gin_ref[:, GLA_QK:2 * GLA_QK]
    v = gin_ref[:, 2 * GLA_QK:2 * GLA_QK + GLA_V]
    la = gin_ref[:, C_GA:GLA_IN_W]

    rowmod = lax.broadcasted_iota(jnp.int32, (rows, GLA_QK), 0) % GLA_SUB
    b = la
    shift = 1
    while shift < GLA_SUB:
        b = b + jnp.where(rowmod >= shift, pltpu.roll(b, shift, axis=0), 0.0)
        shift *= 2
    b_last = jnp.broadcast_to(b.reshape(rows // GLA_SUB, GLA_SUB, GLA_QK)[:, GLA_SUB - 1:GLA_SUB, :],
                              (rows // GLA_SUB, GLA_SUB, GLA_QK)).reshape(rows, GLA_QK)
    eb = jnp.exp(b)
    qb_ref[...] = (q * eb).astype(BF16)
    ke_ref[...] = (k * jnp.exp(b_last - b)).astype(BF16)
    eb_ref[...] = eb

    kpad_ref[HALO:HALO + rows, :] = k
    vpad_ref[HALO:HALO + rows, :] = v
    bpad_ref[HALO:HALO + rows, :] = b

    head_sum = (lax.broadcasted_iota(jnp.int32, (GLA_QK, GLA_V), 0) // GLA_DK ==
                lax.broadcasted_iota(jnp.int32, (GLA_QK, GLA_V), 1) // GLA_DV).astype(BF16)
    o = jnp.zeros((rows, GLA_V), F32)
    for d in range(GLA_SUB):
        lo = HALO - d
        kd = kpad_ref[lo:lo + rows, :]
        bd = bpad_ref[lo:lo + rows, :]
        vd = vpad_ref[lo:lo + rows, :]
        decay = jnp.exp(jnp.where(rowmod >= d, b - bd, NEG_BIG))
        p = (q * kd * decay).astype(BF16)
        o = o + jnp.dot(p, head_sum, preferred_element_type=F32) * vd
    o_ref[...] = o

    state_mask = (lax.broadcasted_iota(jnp.int32, (GLA_V, GLA_QK), 0) // GLA_DV ==
                  lax.broadcasted_iota(jnp.int32, (GLA_V, GLA_QK), 1) // GLA_DK)

    def step(s, state):
        r0 = pl.multiple_of(s * GLA_SUB, GLA_SUB)
        blk = pl.ds(r0, GLA_SUB)
        o_ref[blk, :] += lax.dot_general(qb_ref[blk, :], state.astype(BF16),
                                         (((1,), (1,)), ((), ())), preferred_element_type=F32)
        v_b = gin_ref[blk, 2 * GLA_QK:2 * GLA_QK + GLA_V].astype(BF16)
        kv = lax.dot_general(v_b, ke_ref[blk, :], (((0,), (0,)), ((), ())),
                             preferred_element_type=F32)
        decay_row = eb_ref[pl.ds(r0 + GLA_SUB - 1, 1), :]
        return state * decay_row + jnp.where(state_mask, kv, 0.0)

    state_ref[...] = lax.fori_loop(0, rows // GLA_SUB, step, state_ref[...])

    o = o_ref[...]
    head_mean = (lax.broadcasted_iota(jnp.int32, (GLA_V, GLA_V), 0) // GLA_DV ==
                 lax.broadcasted_iota(jnp.int32, (GLA_V, GLA_V), 1) // GLA_DV).astype(F32) * (1.0 / GLA_DV)
    ms = jnp.dot(o * o, head_mean, precision=HIGHEST, preferred_element_type=F32)
    r = gin_ref[:, 2 * GLA_QK + GLA_V:C_GA]
    out_ref[...] = o * lax.rsqrt(ms + EPS) * gnorm_ref[...] * (r * _sigmoid(r))


def _gla(gla_in, gnorm):
    seq = gla_in.shape[0]
    return pl.pallas_call(
        _gla_kernel,
        grid=(seq // GLA_TILE,),
        in_specs=[pl.BlockSpec((GLA_TILE, GLA_IN_W), lambda i: (i, 0)), _const_spec((1, GLA_V))],
        out_specs=pl.BlockSpec((GLA_TILE, GLA_V), lambda i: (i, 0)),
        out_shape=jax.ShapeDtypeStruct((seq, GLA_V), F32),
        scratch_shapes=[pltpu.VMEM((GLA_V, GLA_QK), F32),
                        pltpu.VMEM((HALO + GLA_TILE, GLA_QK), F32),
                        pltpu.VMEM((HALO + GLA_TILE, GLA_V), F32),
                        pltpu.VMEM((HALO + GLA_TILE, GLA_QK), F32),
                        pltpu.VMEM((GLA_TILE, GLA_QK), BF16),
                        pltpu.VMEM((GLA_TILE, GLA_QK), BF16),
                        pltpu.VMEM((GLA_TILE, GLA_QK), F32),
                        pltpu.VMEM((GLA_TILE, GLA_V), F32)],
        compiler_params=_params("arbitrary"),
        name="gla",
    )(gla_in, gnorm)


def _local_kernel(loc_ref, halo_ref, wpool_ref, pscale_ref, gsgu_ref, wsgu_ref, bsgu_ref,
                  pool_ref, sgu_ref, zpad_ref):
    rows = loc_ref.shape[0]
    i = pl.program_id(0)

    z = loc_ref[:, 0:BRANCH_W]
    zpad_ref[0:HALO, :] = jnp.where(i > 0, halo_ref[...], 0.0)
    zpad_ref[HALO:HALO + rows, :] = z
    t = i * rows + lax.broadcasted_iota(jnp.int32, (rows, BRANCH_W), 0)
    group = lax.broadcasted_iota(jnp.int32, (rows, BRANCH_W), 1) // POOL_GW
    acc = z
    mean = jnp.zeros_like(z)
    width = 1
    for g, w in enumerate(POOL_WINDOWS):
        for back in range(width, w):
            acc = acc + zpad_ref[HALO - back:HALO - back + rows, :]
        width = w
        count = jnp.minimum(t + 1, w).astype(F32)
        mean = jnp.where(group == g, acc / count, mean)
    mixed = (mean - z).astype(BF16)
    pool_ref[...] = jnp.dot(mixed, wpool_ref[...], preferred_element_type=F32) * pscale_ref[...]

    zs = loc_ref[:, BRANCH_W:LOC_W]
    gz = 0.5 * zs * (1.0 + jnp.tanh(np.sqrt(2.0 / np.pi).astype(np.float32) * (zs + 0.044715 * (zs * zs * zs))))
    u = gz[:, 0:BRANCH_W]
    vn = _rms(gz[:, BRANCH_W:], gsgu_ref[...]).astype(BF16)
    tril = (lax.broadcasted_iota(jnp.int32, (SGU_CHUNK, SGU_CHUNK), 0) >=
            lax.broadcasted_iota(jnp.int32, (SGU_CHUNK, SGU_CHUNK), 1))
    w_low = [jnp.where(tril, wsgu_ref[g], 0.0).astype(BF16) for g in range(SGU_GROUPS)]
    cgroup = lax.broadcasted_iota(jnp.int32, (SGU_CHUNK, BRANCH_W), 1) // SGU_GW
    for c in range(rows // SGU_CHUNK):
        sl = slice(c * SGU_CHUNK, (c + 1) * SGU_CHUNK)
        sv = bsgu_ref[...]
        for g in range(SGU_GROUPS):
            mix = jnp.dot(w_low[g], vn[sl], preferred_element_type=F32)
            sv = sv + jnp.where(cgroup == g, mix, 0.0)
        sgu_ref[sl, :] = u[sl] * sv


def _local(loc, wpool_bd, pscale, gsgu, wsgu, bsgu_full):
    seq = loc.shape[0]
    per = ROW_TILE // HALO
    return pl.pallas_call(
        _local_kernel,
        grid=(seq // ROW_TILE,),
        in_specs=[pl.BlockSpec((ROW_TILE, LOC_W), lambda i: (i, 0)),
                  pl.BlockSpec((HALO, BRANCH_W), lambda i: (jnp.maximum(i * per - 1, 0), 0)),
                  _const_spec((BRANCH_W, BRANCH_W)), _const_spec((1, BRANCH_W)), _const_spec((1, BRANCH_W)),
                  _const_spec((SGU_GROUPS, SGU_CHUNK, SGU_CHUNK)), _const_spec((SGU_CHUNK, BRANCH_W))],
        out_specs=[pl.BlockSpec((ROW_TILE, BRANCH_W), lambda i: (i, 0))] * 2,
        out_shape=[jax.ShapeDtypeStruct((seq, BRANCH_W), F32)] * 2,
        scratch_shapes=[pltpu.VMEM((HALO + ROW_TILE, BRANCH_W), F32)],
        compiler_params=_params("parallel"),
        name="local_mixers",
    )(loc, loc, wpool_bd, pscale, gsgu, wsgu, bsgu_full)


def _moba_kernel(qt_ref, k_ref, vt_ref, kmean_ref, o_ref, sel_ref, qtb_ref, m_ref, acc_ref, out_ref, s_ref):
    own = pl.program_id(0)
    nblk = k_ref.shape[0]
    blk_id = lax.broadcasted_iota(jnp.int32, (nblk, MOBA_BLOCK), 0)
    pair_row = lax.broadcasted_iota(jnp.int32, (LANES, MOBA_BLOCK), 0) // MOBA_DH
    key_pos = lax.broadcasted_iota(jnp.int32, (MOBA_BLOCK, MOBA_BLOCK), 0)
    q_pos = lax.broadcasted_iota(jnp.int32, (MOBA_BLOCK, MOBA_BLOCK), 1)
    ones_rows = (lax.broadcasted_iota(jnp.int32, (MOBA_PV_ROWS - MOBA_DH, MOBA_BLOCK), 0) == 0).astype(BF16)

    for h in range(MOBA_HEADS):
        lanes = pl.ds((h // 2) * LANES, LANES)
        qt = jnp.where(pair_row == (h % 2), qt_ref[0, (h // 2) * LANES:(h // 2 + 1) * LANES, :], 0.0)
        qtb_ref[h] = qt.astype(BF16)

        gate = jnp.dot(kmean_ref[:, lanes], qt, precision=HIGHEST, preferred_element_type=F32)
        gate = jnp.where(blk_id < own, gate, -jnp.inf)
        sel = jnp.zeros((nblk, MOBA_BLOCK), F32)
        for _ in range(MOBA_TOPK):
            best = jnp.max(gate, axis=0, keepdims=True)
            first = jnp.min(jnp.where(gate == best, blk_id, nblk), axis=0, keepdims=True)
            pick = (blk_id == first) & (best > -jnp.inf)
            sel = jnp.where(pick, 1.0, sel)
            gate = jnp.where(pick, -jnp.inf, gate)
        sel_ref[h] = sel
        m_ref[h] = jnp.full((1, MOBA_BLOCK), NEG_BIG, F32)
        acc_ref[h] = jnp.zeros((MOBA_PV_ROWS, MOBA_BLOCK), F32)

    def scores(j, h):
        lanes = pl.ds((h // 2) * LANES, LANES)
        s_ref[h] = jnp.dot(k_ref[j, :, lanes], qtb_ref[h], preferred_element_type=F32)

    def attend(j, h, selected, causal):
        s = s_ref[h]
        if causal:
            s = jnp.where(key_pos <= q_pos, s, NEG_BIG)
        m_run = m_ref[h]
        blk_max = jnp.max(jnp.max(s.reshape(8, MOBA_BLOCK // 8, MOBA_BLOCK), axis=0), axis=0, keepdims=True)
        m_new = jnp.maximum(m_run, jnp.where(selected, blk_max, NEG_BIG))
        p = jnp.exp(s - jnp.where(selected, m_new, -NEG_BIG))
        v_aug = jnp.concatenate([vt_ref[j, pl.ds(h * MOBA_DH, MOBA_DH), :], ones_rows], axis=0)
        pv = jnp.dot(v_aug, p.astype(BF16), preferred_element_type=F32)
        acc_ref[h] = acc_ref[h] * jnp.exp(m_run - m_new) + pv
        m_ref[h] = m_new

    def past(j, carry):
        scores(j, 2)
        scores(j, 3)
        attend(j, 0, sel_ref[0, pl.ds(j, 1), :] > 0.0, False)
        attend(j, 1, sel_ref[1, pl.ds(j, 1), :] > 0.0, False)
        scores(j + 1, 0)
        scores(j + 1, 1)
        attend(j, 2, sel_ref[2, pl.ds(j, 1), :] > 0.0, False)
        attend(j, 3, sel_ref[3, pl.ds(j, 1), :] > 0.0, False)
        return carry

    def past_two(t, carry):
        return past(2 * t + 1, past(2 * t, carry))

    scores(0, 0)
    scores(0, 1)
    lax.fori_loop(0, own // 2, past_two, 0)

    @pl.when(own % 2 == 1)
    def _():
        past(own - 1, 0)

    everyone = jnp.full((1, MOBA_BLOCK), True)
    scores(own, 2)
    scores(own, 3)
    for h in range(MOBA_HEADS):
        attend(own, h, everyone, True)
        out_ref[h * MOBA_DH:(h + 1) * MOBA_DH, :] = acc_ref[h, 0:MOBA_DH, :] / acc_ref[h, MOBA_DH:MOBA_DH + 1, :]
    o_ref[...] = out_ref[...].T


def _moba(qt3, k3, vt3, kmean):
    nblk = k3.shape[0]
    seq = nblk * MOBA_BLOCK
    return pl.pallas_call(
        _moba_kernel,
        grid=(nblk,),
        in_specs=[pl.BlockSpec((1, MOBA_W, MOBA_BLOCK), lambda i: (i, 0, 0)),
                  _const_spec((nblk, MOBA_BLOCK, MOBA_W)), _const_spec((nblk, MOBA_W, MOBA_BLOCK)),
                  _const_spec((nblk, MOBA_W))],
        out_specs=pl.BlockSpec((MOBA_BLOCK, MOBA_W), lambda i: (i, 0)),
        out_shape=jax.ShapeDtypeStruct((seq, MOBA_W), F32),
        scratch_shapes=[pltpu.VMEM((MOBA_HEADS, nblk, MOBA_BLOCK), F32),
                        pltpu.VMEM((MOBA_HEADS, LANES, MOBA_BLOCK), BF16),
                        pltpu.VMEM((MOBA_HEADS, 1, MOBA_BLOCK), F32),
                        pltpu.VMEM((MOBA_HEADS, MOBA_PV_ROWS, MOBA_BLOCK), F32),
                        pltpu.VMEM((MOBA_W, MOBA_BLOCK), F32),
                        pltpu.VMEM((MOBA_HEADS, MOBA_BLOCK, MOBA_BLOCK), F32)],
        compiler_params=_params("parallel"),
        name="moba",
    )(qt3, k3, vt3, kmean)


def _merge_kernel(x_ref, g_ref, gla_ref, pool_ref, sgu_ref, moba_ref, wgate_ref, wbr_ref, wout_ref, y_ref):
    x = x_ref[...]
    h = _rms(x, g_ref[...]).astype(BF16)
    merged = None
    for b, br_ref in enumerate((gla_ref, pool_ref, sgu_ref, moba_ref)):
        gate = _sigmoid(jnp.dot(h, wgate_ref[:, b * D_MODEL:(b + 1) * D_MODEL], preferred_element_type=F32))
        term = gate * jnp.dot(br_ref[...].astype(BF16), wbr_ref[b], preferred_element_type=F32)
        merged = term if merged is None else merged + term
    y_ref[...] = x + jnp.dot(merged.astype(BF16), wout_ref[...], preferred_element_type=F32)


def _merge(x, g, branches, wgate, wbr, wout):
    seq = x.shape[0]
    row = lambda w: pl.BlockSpec((ROW_TILE, w), lambda i: (i, 0))
    return pl.pallas_call(
        _merge_kernel,
        grid=(seq // ROW_TILE,),
        in_specs=[row(D_MODEL), _const_spec((1, D_MODEL))] + [row(BRANCH_W)] * N_BRANCH +
                 [_const_spec((D_MODEL, N_BRANCH * D_MODEL)), _const_spec((N_BRANCH, BRANCH_W, D_MODEL)),
                  _const_spec((D_MODEL, D_MODEL))],
        out_specs=row(D_MODEL),
        out_shape=jax.ShapeDtypeStruct((seq, D_MODEL), F32),
        compiler_params=_params("parallel"),
        name="merge",
    )(x, g, *branches, wgate, wbr, wout)


def _ffn_kernel(x_ref, halo_ref, g_ref, wup_ref, cw_ref, cb_ref, wdown_ref, gfin_ref, y_ref,
                h_ref, acc_ref, *, final_norm):
    rows = x_ref.shape[0]
    i = pl.program_id(0)
    x = x_ref[...]
    h_ref[0:HALO, :] = _rms(halo_ref[...], g_ref[...]).astype(BF16)
    h_ref[HALO:HALO + rows, :] = _rms(x, g_ref[...]).astype(BF16)
    padded_row = lax.broadcasted_iota(jnp.int32, (HALO + rows, FF_CHUNK), 0)
    before_start = (padded_row < HALO) & (i == 0)

    def conv(col0):
        up = jnp.dot(h_ref[...], wup_ref[:, pl.ds(col0, FF_CHUNK)], preferred_element_type=F32)
        up = jnp.where(before_start, 0.0, up)
        cw = cw_ref[:, pl.ds(col0, FF_CHUNK)]
        out = (pltpu.roll(up, 2, axis=0) * cw[0:1] + pltpu.roll(up, 1, axis=0) * cw[1:2] + up * cw[2:3]
               + cb_ref[:, pl.ds(col0, FF_CHUNK)])
        return out[HALO:]

    acc_ref[...] = x
    for c in range(D_FF // FF_CHUNK):
        a = conv(c * FF_CHUNK)
        b = conv(D_FF + c * FF_CHUNK)
        act = (a * _sigmoid(a) * b).astype(BF16)
        acc_ref[...] += jnp.dot(act, wdown_ref[c * FF_CHUNK:(c + 1) * FF_CHUNK, :], preferred_element_type=F32)
    y = acc_ref[...]
    y_ref[...] = _rms(y, gfin_ref[...]) if final_norm else y


def _ffn(x, g, wup, cw, cb, wdown, gfin, final_norm):
    seq = x.shape[0]
    per = ROW_TILE // HALO
    row = pl.BlockSpec((ROW_TILE, D_MODEL), lambda i: (i, 0))
    return pl.pallas_call(
        functools.partial(_ffn_kernel, final_norm=final_norm),
        grid=(seq // ROW_TILE,),
        in_specs=[row, pl.BlockSpec((HALO, D_MODEL), lambda i: (jnp.maximum(i * per - 1, 0), 0)),
                  _const_spec((1, D_MODEL)), _const_spec((D_MODEL, 2 * D_FF)), _const_spec((3, 2 * D_FF)),
                  _const_spec((1, 2 * D_FF)), _const_spec((D_FF, D_MODEL)), _const_spec((1, D_MODEL))],
        out_specs=row,
        out_shape=jax.ShapeDtypeStruct((seq, D_MODEL), F32),
        scratch_shapes=[pltpu.VMEM((HALO + ROW_TILE, D_MODEL), BF16), pltpu.VMEM((ROW_TILE, D_MODEL), F32)],
        compiler_params=_params("parallel"),
        name="conv_ffn",
    )(x, x, g, wup, cw, cb, wdown, gfin)


def _pack_w_in(w_in):
    n_front = 2 * GLA_QK + 2 * GLA_V
    pad = jnp.zeros((w_in.shape[0], LANES - GLA_RANK), w_in.dtype)
    return jnp.concatenate([w_in[:, :n_front + GLA_RANK], pad, w_in[:, n_front + GLA_RANK:]], axis=1).astype(BF16)


def _block_diag(w):
    g, a, b = w.shape
    eye = jnp.eye(g, dtype=w.dtype)
    return (eye[:, None, :, None] * w[:, :, None, :]).reshape(g * a, g * b)


def kernel(x, positions, g_mix, w_in, w_gla_a, b_gla_a, g_gla_norm, w_pool, pool_scale, g_sgu, w_sgu, b_sgu,
           w_gate, w_branch, w_out, g_ffn, w_up, conv_w, conv_b, w_down, g_final):
    bsz, seq, d = x.shape
    assert bsz == 1 and d == D_MODEL and seq % ROW_TILE == 0 and seq % GLA_TILE == 0
    depth = w_in.shape[0]
    xs = x[0]
    pos = positions.reshape(seq, 1)
    half = MOBA_DH // 2
    inv_freq = ROPE_THETA ** (-jnp.arange(half, dtype=F32) / half)
    invf = jnp.tile(inv_freq, LANES // half)[None, :]
    row = lambda a: a.reshape(1, -1)

    for l in range(depth):
        wa2 = jnp.concatenate([w_gla_a[l], jnp.zeros((LANES - GLA_RANK, GLA_QK), F32)], axis=0)
        gla_in, loc, qt3, k, vt3, kmean = _inproj(xs, pos, row(g_mix[l]), _pack_w_in(w_in[l]), wa2,
                                                  row(b_gla_a[l]), invf)
        o_gla = _gla(gla_in, row(jnp.tile(g_gla_norm[l], GLA_HEADS)))
        o_pool, o_sgu = _local(loc, _block_diag(w_pool[l]).astype(BF16), row(pool_scale[l]), row(g_sgu[l]),
                               w_sgu[l], jnp.repeat(b_sgu[l].T, SGU_GW, axis=1))
        o_moba = _moba(qt3, k.reshape(seq // MOBA_BLOCK, MOBA_BLOCK, MOBA_W), vt3,
                       kmean.reshape(seq // MOBA_BLOCK, MOBA_W))
        xs = _merge(xs, row(g_mix[l]), (o_gla, o_pool, o_sgu, o_moba), w_gate[l].astype(BF16),
                    w_branch[l].astype(BF16), w_out[l].astype(BF16))
        xs = _ffn(xs, row(g_ffn[l]), w_up[l].astype(BF16), conv_w[l], row(conv_b[l]), w_down[l].astype(BF16),
                  row(g_final), final_norm=(l == depth - 1))
    return xs[None]
```

```python
import functools

import numpy as np
import jax
import jax.numpy as jnp
from jax import lax
from jax.experimental import pallas as pl
from jax.experimental.pallas import tpu as pltpu

F32 = jnp.float32
BF16 = jnp.bfloat16
HIGHEST = lax.Precision.HIGHEST

D_MODEL = 1024
N_BRANCH = 4
BRANCH_W = 256
GLA_HEADS, GLA_DK, GLA_DV, GLA_RANK, GLA_TAU = 4, 32, 64, 16, 16.0
GLA_QK = GLA_HEADS * GLA_DK
GLA_V = GLA_HEADS * GLA_DV
POOL_WINDOWS = (2, 4, 8, 16)
POOL_GW = BRANCH_W // len(POOL_WINDOWS)
SGU_CHUNK, SGU_GROUPS = 128, 4
SGU_GW = BRANCH_W // SGU_GROUPS
MOBA_HEADS, MOBA_DH, MOBA_BLOCK, MOBA_TOPK = 4, 64, 256, 3
MOBA_W = MOBA_HEADS * MOBA_DH
MOBA_PV_ROWS = MOBA_DH + 16
ROPE_THETA = 10000.0
D_FF = 2816
EPS = 1e-6

LANES = 128
VMEM_LIMIT_BYTES = 56 * 1024 * 1024

ROW_TILE = 512
GLA_TILE = 256
GLA_SUB = 16
HALO = 16
FF_CHUNK = 256
NEG_BIG = -1e30
LOG2_E = float(np.log2(np.e))

C_GLA = 0
C_GA = C_GLA + 2 * GLA_QK + 2 * GLA_V
C_LOC = C_GA + LANES
C_MQ = C_LOC + BRANCH_W + 2 * BRANCH_W
C_MK = C_MQ + MOBA_W
C_MV = C_MK + MOBA_W
C_END = C_MV + MOBA_W
GLA_IN_W = C_GA + GLA_QK
LOC_W = 3 * BRANCH_W


def _rms(x, g):
    return x * lax.rsqrt(jnp.mean(x * x, axis=-1, keepdims=True) + EPS) * g


def _sigmoid(x):
    return 1.0 / (1.0 + jnp.exp(-x))


def _params(*sem):
    return pltpu.CompilerParams(dimension_semantics=sem, vmem_limit_bytes=VMEM_LIMIT_BYTES)


def _const_spec(shape):
    return pl.BlockSpec(shape, lambda i: (0,) * len(shape))


def _rope_tile(x, cos, sin, first_half):
    partner = jnp.where(first_half, -pltpu.roll(x, LANES - MOBA_DH // 2, axis=1),
                        pltpu.roll(x, MOBA_DH // 2, axis=1))
    return x * cos + partner * sin


def _inproj_kernel(x_ref, pos_ref, g_ref, w_ref, wa2_ref, ba_ref, invf_ref,
                   gla_ref, loc_ref, qt_ref, k_ref, vt_ref, kmean_ref):
    rows = x_ref.shape[0]
    h = _rms(x_ref[...], g_ref[...]).astype(BF16)
    proj = jnp.dot(h, w_ref[...], preferred_element_type=F32)
    proj_m = proj[:, C_MQ:C_END]
    loc_ref[...] = proj[:, C_LOC:C_MQ]

    ang = pos_ref[...].astype(F32) * invf_ref[...]
    cos, sin = jnp.cos(ang), jnp.sin(ang)
    lane = lax.broadcasted_iota(jnp.int32, (rows, LANES), 1)
    first_half = (lane % MOBA_DH) < (MOBA_DH // 2)

    def roped(c0):
        return jnp.concatenate(
            [_rope_tile(proj_m[:, c0 + t * LANES:c0 + (t + 1) * LANES], cos, sin, first_half)
             for t in range(MOBA_W // LANES)], axis=1)

    q = roped(0) * (MOBA_DH ** -0.5)
    k = roped(MOBA_W)
    v = proj_m[:, 2 * MOBA_W:]

    proj_g = proj[:, C_GLA:C_LOC]
    gla_ref[:, 0:C_GA] = proj_g[:, 0:C_GA]
    logit = jnp.dot(proj_g[:, C_GA:C_LOC], wa2_ref[...], precision=HIGHEST,
                    preferred_element_type=F32) + ba_ref[...]
    log_sig = jnp.minimum(logit, 0.0) - jnp.log(1.0 + jnp.exp(-jnp.abs(logit)))
    gla_ref[:, C_GA:GLA_IN_W] = log_sig * (1.0 / GLA_TAU)

    k_ref[...] = k.astype(BF16)
    for r in range(rows // MOBA_BLOCK):
        sl = slice(r * MOBA_BLOCK, (r + 1) * MOBA_BLOCK)
        qt_ref[r] = q[sl].T
        vt_ref[r] = v[sl].T.astype(BF16)
        kmean_ref[0, r:r + 1, :] = jnp.mean(k[sl], axis=0, keepdims=True)


def _inproj(x, pos, g, w_cat, wa2, ba, invf):
    seq = x.shape[0]
    nblk = ROW_TILE // MOBA_BLOCK
    row = lambda w: pl.BlockSpec((ROW_TILE, w), lambda i: (i, 0))
    blk3 = pl.BlockSpec((nblk, MOBA_W, MOBA_BLOCK), lambda i: (i, 0, 0))
    return pl.pallas_call(
        _inproj_kernel,
        grid=(seq // ROW_TILE,),
        in_specs=[row(D_MODEL), row(1), _const_spec((1, D_MODEL)), _const_spec((D_MODEL, C_END)),
                  _const_spec((LANES, GLA_QK)), _const_spec((1, GLA_QK)), _const_spec((1, LANES))],
        out_specs=[row(GLA_IN_W), row(LOC_W), blk3, row(MOBA_W), blk3,
                   pl.BlockSpec((1, nblk, MOBA_W), lambda i: (i, 0, 0))],
        out_shape=[jax.ShapeDtypeStruct((seq, GLA_IN_W), F32),
                   jax.ShapeDtypeStruct((seq, LOC_W), F32),
                   jax.ShapeDtypeStruct((seq // MOBA_BLOCK, MOBA_W, MOBA_BLOCK), F32),
                   jax.ShapeDtypeStruct((seq, MOBA_W), BF16),
                   jax.ShapeDtypeStruct((seq // MOBA_BLOCK, MOBA_W, MOBA_BLOCK), BF16),
                   jax.ShapeDtypeStruct((seq // ROW_TILE, nblk, MOBA_W), F32)],
        compiler_params=_params("parallel"),
        name="inproj",
    )(x, pos, g, w_cat, wa2, ba, invf)


def _gla_kernel(gin_ref, gnorm_ref, out_ref, state_ref, kpad_ref, vpad_ref, bpad_ref,
                qb_ref, ke_ref, eb_ref, o_ref, kv_ref):
    rows = gin_ref.shape[0]

    @pl.when(pl.program_id(0) == 0)
    def _():
        state_ref[...] = jnp.zeros_like(state_ref)
        kpad_ref[0:HALO, :] = jnp.zeros((HALO, GLA_QK), F32)
        vpad_ref[0:HALO, :] = jnp.zeros((HALO, GLA_V), F32)
        bpad_ref[0:HALO, :] = jnp.zeros((HALO, GLA_QK), F32)

    q = gin_ref[:, 0:GLA_QK] * (GLA_DK ** -0.5)
    k = gin_ref[:, GLA_QK:2 * GLA_QK]
    v = gin_ref[:, 2 * GLA_QK:2 * GLA_QK + GLA_V]
    la = gin_ref[:, C_GA:GLA_IN_W]

    rowmod = lax.broadcasted_iota(jnp.int32, (rows, GLA_QK), 0) % GLA_SUB
    b = la
    shift = 1
    while shift < GLA_SUB:
        b = b + jnp.where(rowmod >= shift, pltpu.roll(b, shift, axis=0), 0.0)
        shift *= 2
    b_last = jnp.broadcast_to(b.reshape(rows // GLA_SUB, GLA_SUB, GLA_QK)[:, GLA_SUB - 1:GLA_SUB, :],
                              (rows // GLA_SUB, GLA_SUB, GLA_QK)).reshape(rows, GLA_QK)
    eb = jnp.exp(b)
    qb_ref[...] = (q * eb).astype(BF16)
    ke_ref[...] = (k * jnp.exp(b_last - b)).astype(BF16)
    eb_ref[...] = eb

    kpad_ref[HALO:HALO + rows, :] = k
    vpad_ref[HALO:HALO + rows, :] = v
    bpad_ref[HALO:HALO + rows, :] = b

    head_sum = (lax.broadcasted_iota(jnp.int32, (GLA_QK, GLA_V), 0) // GLA_DK ==
                lax.broadcasted_iota(jnp.int32, (GLA_QK, GLA_V), 1) // GLA_DV).astype(BF16)
    o = jnp.zeros((rows, GLA_V), F32)
    for d in range(GLA_SUB):
        lo = HALO - d
        kd = kpad_ref[lo:lo + rows, :]
        bd = bpad_ref[lo:lo + rows, :]
        vd = vpad_ref[lo:lo + rows, :]
        decay = jnp.exp(jnp.where(rowmod >= d, b - bd, NEG_BIG))
        p = (q * kd * decay).astype(BF16)
        o = o + jnp.dot(p, head_sum, preferred_element_type=F32) * vd
    o_ref[...] = o

    state_mask = (lax.broadcasted_iota(jnp.int32, (GLA_V, GLA_QK), 0) // GLA_DV ==
                  lax.broadcasted_iota(jnp.int32, (GLA_V, GLA_QK), 1) // GLA_DK)

    steps = rows // GLA_SUB
    for s in range(steps):
        blk = slice(s * GLA_SUB, (s + 1) * GLA_SUB)
        v_b = gin_ref[blk, 2 * GLA_QK:2 * GLA_QK + GLA_V].astype(BF16)
        kv = lax.dot_general(v_b, ke_ref[blk, :], (((0,), (0,)), ((), ())),
                             preferred_element_type=F32)
        kv_ref[s] = jnp.where(state_mask, kv, 0.0)

    state = state_ref[...]
    for s in range(steps):
        blk = slice(s * GLA_SUB, (s + 1) * GLA_SUB)
        o_ref[blk, :] += lax.dot_general(qb_ref[blk, :], state.astype(BF16),
                                         (((1,), (1,)), ((), ())), preferred_element_type=F32)
        state = state * eb_ref[(s + 1) * GLA_SUB - 1:(s + 1) * GLA_SUB, :] + kv_ref[s]
    state_ref[...] = state

    o = o_ref[...]
    head_mean = (lax.broadcasted_iota(jnp.int32, (GLA_V, GLA_V), 0) // GLA_DV ==
                 lax.broadcasted_iota(jnp.int32, (GLA_V, GLA_V), 1) // GLA_DV).astype(F32) * (1.0 / GLA_DV)
    ms = jnp.dot(o * o, head_mean, precision=HIGHEST, preferred_element_type=F32)
    r = gin_ref[:, 2 * GLA_QK + GLA_V:C_GA]
    out_ref[...] = o * lax.rsqrt(ms + EPS) * gnorm_ref[...] * (r * _sigmoid(r))


def _gla(gla_in, gnorm):
    seq = gla_in.shape[0]
    return pl.pallas_call(
        _gla_kernel,
        grid=(seq // GLA_TILE,),
        in_specs=[pl.BlockSpec((GLA_TILE, GLA_IN_W), lambda i: (i, 0)), _const_spec((1, GLA_V))],
        out_specs=pl.BlockSpec((GLA_TILE, GLA_V), lambda i: (i, 0)),
        out_shape=jax.ShapeDtypeStruct((seq, GLA_V), F32),
        scratch_shapes=[pltpu.VMEM((GLA_V, GLA_QK), F32),
                        pltpu.VMEM((HALO + GLA_TILE, GLA_QK), F32),
                        pltpu.VMEM((HALO + GLA_TILE, GLA_V), F32),
                        pltpu.VMEM((HALO + GLA_TILE, GLA_QK), F32),
                        pltpu.VMEM((GLA_TILE, GLA_QK), BF16),
                        pltpu.VMEM((GLA_TILE, GLA_QK), BF16),
                        pltpu.VMEM((GLA_TILE, GLA_QK), F32),
                        pltpu.VMEM((GLA_TILE, GLA_V), F32),
                        pltpu.VMEM((GLA_TILE // GLA_SUB, GLA_V, GLA_QK), F32)],
        compiler_params=_params("arbitrary"),
        name="gla",
    )(gla_in, gnorm)


def _local_kernel(loc_ref, halo_ref, wpool_ref, pscale_ref, gsgu_ref, wsgu_ref, bsgu_ref,
                  pool_ref, sgu_ref, zpad_ref):
    rows = loc_ref.shape[0]
    i = pl.program_id(0)

    z = loc_ref[:, 0:BRANCH_W]
    zpad_ref[0:HALO, :] = jnp.where(i > 0, halo_ref[...], 0.0)
    zpad_ref[HALO:HALO + rows, :] = z
    t = i * rows + lax.broadcasted_iota(jnp.int32, (rows, BRANCH_W), 0)
    group = lax.broadcasted_iota(jnp.int32, (rows, BRANCH_W), 1) // POOL_GW
    acc = z
    mean = jnp.zeros_like(z)
    width = 1
    for g, w in enumerate(POOL_WINDOWS):
        for back in range(width, w):
            acc = acc + zpad_ref[HALO - back:HALO - back + rows, :]
        width = w
        count = jnp.minimum(t + 1, w).astype(F32)
        mean = jnp.where(group == g, acc / count, mean)
    mixed = (mean - z).astype(BF16)
    pool_ref[...] = jnp.dot(mixed, wpool_ref[...], preferred_element_type=F32) * pscale_ref[...]

    zs = loc_ref[:, BRANCH_W:LOC_W]
    gz = 0.5 * zs * (1.0 + jnp.tanh(np.sqrt(2.0 / np.pi).astype(np.float32) * (zs + 0.044715 * (zs * zs * zs))))
    u = gz[:, 0:BRANCH_W]
    vn = _rms(gz[:, BRANCH_W:], gsgu_ref[...]).astype(BF16)
    tril = (lax.broadcasted_iota(jnp.int32, (SGU_CHUNK, SGU_CHUNK), 0) >=
            lax.broadcasted_iota(jnp.int32, (SGU_CHUNK, SGU_CHUNK), 1))
    w_low = [jnp.where(tril, wsgu_ref[g], 0.0).astype(BF16) for g in range(SGU_GROUPS)]
    cgroup = lax.broadcasted_iota(jnp.int32, (SGU_CHUNK, BRANCH_W), 1) // SGU_GW
    for c in range(rows // SGU_CHUNK):
        sl = slice(c * SGU_CHUNK, (c + 1) * SGU_CHUNK)
        sv = bsgu_ref[...]
        for g in range(SGU_GROUPS):
            mix = jnp.dot(w_low[g], vn[sl], preferred_element_type=F32)
            sv = sv + jnp.where(cgroup == g, mix, 0.0)
        sgu_ref[sl, :] = u[sl] * sv


def _local(loc, wpool_bd, pscale, gsgu, wsgu, bsgu_full):
    seq = loc.shape[0]
    per = ROW_TILE // HALO
    return pl.pallas_call(
        _local_kernel,
        grid=(seq // ROW_TILE,),
        in_specs=[pl.BlockSpec((ROW_TILE, LOC_W), lambda i: (i, 0)),
                  pl.BlockSpec((HALO, BRANCH_W), lambda i: (jnp.maximum(i * per - 1, 0), 0)),
                  _const_spec((BRANCH_W, BRANCH_W)), _const_spec((1, BRANCH_W)), _const_spec((1, BRANCH_W)),
                  _const_spec((SGU_GROUPS, SGU_CHUNK, SGU_CHUNK)), _const_spec((SGU_CHUNK, BRANCH_W))],
        out_specs=[pl.BlockSpec((ROW_TILE, BRANCH_W), lambda i: (i, 0))] * 2,
        out_shape=[jax.ShapeDtypeStruct((seq, BRANCH_W), F32)] * 2,
        scratch_shapes=[pltpu.VMEM((HALO + ROW_TILE, BRANCH_W), F32)],
        compiler_params=_params("parallel"),
        name="local_mixers",
    )(loc, loc, wpool_bd, pscale, gsgu, wsgu, bsgu_full)


def _moba_kernel(qt_ref, k_ref, vt_ref, kmean_ref, o_ref, sel_ref, qtb_ref, m_ref, acc_ref, out_ref, s_ref):
    own = pl.program_id(0)
    nblk = k_ref.shape[0]
    blk_id = lax.broadcasted_iota(jnp.int32, (nblk, MOBA_BLOCK), 0)
    pair_row = lax.broadcasted_iota(jnp.int32, (LANES, MOBA_BLOCK), 0) // MOBA_DH
    key_pos = lax.broadcasted_iota(jnp.int32, (MOBA_BLOCK, MOBA_BLOCK), 0)
    q_pos = lax.broadcasted_iota(jnp.int32, (MOBA_BLOCK, MOBA_BLOCK), 1)
    ones_rows = (lax.broadcasted_iota(jnp.int32, (MOBA_PV_ROWS - MOBA_DH, MOBA_BLOCK), 0) == 0).astype(BF16)

    for h in range(MOBA_HEADS):
        lanes = pl.ds((h // 2) * LANES, LANES)
        qt = jnp.where(pair_row == (h % 2), qt_ref[0, (h // 2) * LANES:(h // 2 + 1) * LANES, :], 0.0)
        qtb_ref[h] = (qt * LOG2_E).astype(BF16)

        gate = jnp.dot(kmean_ref[:, lanes], qt, precision=HIGHEST, preferred_element_type=F32)
        gate = jnp.where(blk_id < own, gate, -jnp.inf)
        sel = jnp.zeros((nblk, MOBA_BLOCK), F32)
        for _ in range(MOBA_TOPK):
            best = jnp.max(gate, axis=0, keepdims=True)
            first = jnp.min(jnp.where(gate == best, blk_id, nblk), axis=0, keepdims=True)
            pick = (blk_id == first) & (best > -jnp.inf)
            sel = jnp.where(pick, 1.0, sel)
            gate = jnp.where(pick, -jnp.inf, gate)
        sel_ref[h] = sel
        m_ref[h] = jnp.full((1, MOBA_BLOCK), NEG_BIG, F32)
        acc_ref[h] = jnp.zeros((MOBA_PV_ROWS, MOBA_BLOCK), F32)

    def scores(j, h, slot):
        lanes = pl.ds((h // 2) * LANES, LANES)
        s_ref[slot * MOBA_HEADS + h] = jnp.dot(k_ref[j, :, lanes], qtb_ref[h],
                                               preferred_element_type=F32).astype(BF16)

    def attend(j, h, slot, selected, causal):
        s = s_ref[slot * MOBA_HEADS + h]
        if causal:
            s = jnp.where(key_pos <= q_pos, s, NEG_BIG)
        m_run = m_ref[h]
        blk_max = jnp.max(jnp.max(s.reshape(8, MOBA_BLOCK // 8, MOBA_BLOCK), axis=0), axis=0, keepdims=True)
        m_new = jnp.maximum(m_run, jnp.where(selected, blk_max.astype(F32), NEG_BIG))
        shift = jnp.where(selected, m_new, -NEG_BIG).astype(BF16)
        p = jnp.exp2(s - shift)
        v_aug = jnp.concatenate([vt_ref[j, pl.ds(h * MOBA_DH, MOBA_DH), :], ones_rows], axis=0)
        pv = jnp.dot(v_aug, p, preferred_element_type=F32)
        acc_ref[h] = acc_ref[h] * jnp.exp2(m_run - m_new) + pv
        m_ref[h] = m_new

    def past(j, slot):
        for pair in range(MOBA_HEADS // 2):
            scores(j + 1, 2 * pair, 1 - slot)
            scores(j + 1, 2 * pair + 1, 1 - slot)
            for h in (2 * pair, 2 * pair + 1):
                attend(j, h, slot, sel_ref[h, pl.ds(j, 1), :] > 0.0, False)

    def past_two(t, carry):
        past(2 * t, 0)
        past(2 * t + 1, 1)
        return carry

    def own_block(slot):
        everyone = jnp.full((1, MOBA_BLOCK), True)
        for h in range(MOBA_HEADS):
            attend(own, h, slot, everyone, True)

    for h in range(MOBA_HEADS):
        scores(0, h, 0)
    lax.fori_loop(0, own // 2, past_two, 0)

    @pl.when(own % 2 == 0)
    def _():
        own_block(0)

    @pl.when(own % 2 == 1)
    def _():
        past(own - 1, 0)
        own_block(1)

    for h in range(MOBA_HEADS):
        out_ref[h * MOBA_DH:(h + 1) * MOBA_DH, :] = acc_ref[h, 0:MOBA_DH, :] / acc_ref[h, MOBA_DH:MOBA_DH + 1, :]
    o_ref[...] = out_ref[...].T


def _moba(qt3, k3, vt3, kmean):
    nblk = k3.shape[0]
    seq = nblk * MOBA_BLOCK
    return pl.pallas_call(
        _moba_kernel,
        grid=(nblk,),
        in_specs=[pl.BlockSpec((1, MOBA_W, MOBA_BLOCK), lambda i: (i, 0, 0)),
                  _const_spec((nblk, MOBA_BLOCK, MOBA_W)), _const_spec((nblk, MOBA_W, MOBA_BLOCK)),
                  _const_spec((nblk, MOBA_W))],
        out_specs=pl.BlockSpec((MOBA_BLOCK, MOBA_W), lambda i: (i, 0)),
        out_shape=jax.ShapeDtypeStruct((seq, MOBA_W), F32),
        scratch_shapes=[pltpu.VMEM((MOBA_HEADS, nblk, MOBA_BLOCK), F32),
                        pltpu.VMEM((MOBA_HEADS, LANES, MOBA_BLOCK), BF16),
                        pltpu.VMEM((MOBA_HEADS, 1, MOBA_BLOCK), F32),
                        pltpu.VMEM((MOBA_HEADS, MOBA_PV_ROWS, MOBA_BLOCK), F32),
                        pltpu.VMEM((MOBA_W, MOBA_BLOCK), F32),
                        pltpu.VMEM((2 * MOBA_HEADS, MOBA_BLOCK, MOBA_BLOCK), BF16)],
        compiler_params=_params("parallel"),
        name="moba",
    )(qt3, k3, vt3, kmean)


def _merge_kernel(x_ref, g_ref, gla_ref, pool_ref, sgu_ref, moba_ref, wgate_ref, wbr_ref, wout_ref, y_ref):
    x = x_ref[...]
    h = _rms(x, g_ref[...]).astype(BF16)
    merged = None
    for b, br_ref in enumerate((gla_ref, pool_ref, sgu_ref, moba_ref)):
        gate = _sigmoid(jnp.dot(h, wgate_ref[:, b * D_MODEL:(b + 1) * D_MODEL], preferred_element_type=F32))
        term = gate * jnp.dot(br_ref[...].astype(BF16), wbr_ref[b], preferred_element_type=F32)
        merged = term if merged is None else merged + term
    y_ref[...] = x + jnp.dot(merged.astype(BF16), wout_ref[...], preferred_element_type=F32)


def _merge(x, g, branches, wgate, wbr, wout):
    seq = x.shape[0]
    row = lambda w: pl.BlockSpec((ROW_TILE, w), lambda i: (i, 0))
    return pl.pallas_call(
        _merge_kernel,
        grid=(seq // ROW_TILE,),
        in_specs=[row(D_MODEL), _const_spec((1, D_MODEL))] + [row(BRANCH_W)] * N_BRANCH +
                 [_const_spec((D_MODEL, N_BRANCH * D_MODEL)), _const_spec((N_BRANCH, BRANCH_W, D_MODEL)),
                  _const_spec((D_MODEL, D_MODEL))],
        out_specs=row(D_MODEL),
        out_shape=jax.ShapeDtypeStruct((seq, D_MODEL), F32),
        compiler_params=_params("parallel"),
        name="merge",
    )(x, g, *branches, wgate, wbr, wout)


def _ffn_kernel(x_ref, halo_ref, g_ref, wup_ref, cw_ref, cb_ref, wdown_ref, gfin_ref, y_ref,
                h_ref, acc_ref, up_ref, act_ref, *, final_norm):
    rows = x_ref.shape[0]
    x = x_ref[...]
    h_halo = _rms(halo_ref[...], g_ref[...])
    h_ref[0:HALO, :] = jnp.where(pl.program_id(0) > 0, h_halo, 0.0).astype(BF16)
    h_ref[HALO:HALO + rows, :] = _rms(x, g_ref[...]).astype(BF16)
    n_chunks = D_FF // FF_CHUNK

    def up_proj(c, slot):
        for half in range(2):
            cols = pl.ds(half * D_FF + c * FF_CHUNK, FF_CHUNK)
            up_ref[slot, half] = jnp.dot(h_ref[...], wup_ref[:, cols], preferred_element_type=F32)

    def conv(c, slot, half):
        cols = pl.ds(half * D_FF + c * FF_CHUNK, FF_CHUNK)
        cw = cw_ref[:, cols]
        return (up_ref[slot, half, HALO - 2:HALO - 2 + rows, :] * cw[0:1] +
                up_ref[slot, half, HALO - 1:HALO - 1 + rows, :] * cw[1:2] +
                up_ref[slot, half, HALO:HALO + rows, :] * cw[2:3] + cb_ref[:, cols])

    acc_ref[...] = x
    up_proj(0, 0)
    for c in range(n_chunks + 1):
        if c + 1 < n_chunks:
            up_proj(c + 1, (c + 1) % 2)
        if c >= 1:
            acc_ref[...] += jnp.dot(act_ref[(c - 1) % 2], wdown_ref[(c - 1) * FF_CHUNK:c * FF_CHUNK, :],
                                    preferred_element_type=F32)
        if c < n_chunks:
            a = conv(c, c % 2, 0)
            b = conv(c, c % 2, 1)
            act_ref[c % 2] = (a * _sigmoid(a) * b).astype(BF16)
    y = acc_ref[...]
    y_ref[...] = _rms(y, gfin_ref[...]) if final_norm else y


def _ffn(x, g, wup, cw, cb, wdown, gfin, final_norm):
    seq = x.shape[0]
    per = ROW_TILE // HALO
    row = pl.BlockSpec((ROW_TILE, D_MODEL), lambda i: (i, 0))
    return pl.pallas_call(
        functools.partial(_ffn_kernel, final_norm=final_norm),
        grid=(seq // ROW_TILE,),
        in_specs=[row, pl.BlockSpec((HALO, D_MODEL), lambda i: (jnp.maximum(i * per - 1, 0), 0)),
                  _const_spec((1, D_MODEL)), _const_spec((D_MODEL, 2 * D_FF)), _const_spec((3, 2 * D_FF)),
                  _const_spec((1, 2 * D_FF)), _const_spec((D_FF, D_MODEL)), _const_spec((1, D_MODEL))],
        out_specs=row,
        out_shape=jax.ShapeDtypeStruct((seq, D_MODEL), F32),
        scratch_shapes=[pltpu.VMEM((HALO + ROW_TILE, D_MODEL), BF16), pltpu.VMEM((ROW_TILE, D_MODEL), F32),
                        pltpu.VMEM((2, 2, HALO + ROW_TILE, FF_CHUNK), F32),
                        pltpu.VMEM((2, ROW_TILE, FF_CHUNK), BF16)],
        compiler_params=_params("parallel"),
        name="conv_ffn",
    )(x, x, g, wup, cw, cb, wdown, gfin)


def _pack_w_in(w_in):
    n_front = 2 * GLA_QK + 2 * GLA_V
    pad = jnp.zeros((w_in.shape[0], LANES - GLA_RANK), w_in.dtype)
    return jnp.concatenate([w_in[:, :n_front + GLA_RANK], pad, w_in[:, n_front + GLA_RANK:]], axis=1).astype(BF16)


def _block_diag(w):
    g, a, b = w.shape
    eye = jnp.eye(g, dtype=w.dtype)
    return (eye[:, None, :, None] * w[:, :, None, :]).reshape(g * a, g * b)


def kernel(x, positions, g_mix, w_in, w_gla_a, b_gla_a, g_gla_norm, w_pool, pool_scale, g_sgu, w_sgu, b_sgu,
           w_gate, w_branch, w_out, g_ffn, w_up, conv_w, conv_b, w_down, g_final):
    bsz, seq, d = x.shape
    assert bsz == 1 and d == D_MODEL and seq % ROW_TILE == 0 and seq % GLA_TILE == 0
    depth = w_in.shape[0]
    xs = x[0]
    pos = positions.reshape(seq, 1)
    half = MOBA_DH // 2
    inv_freq = ROPE_THETA ** (-jnp.arange(half, dtype=F32) / half)
    invf = jnp.tile(inv_freq, LANES // half)[None, :]
    row = lambda a: a.reshape(1, -1)

    for l in range(depth):
        wa2 = jnp.concatenate([w_gla_a[l], jnp.zeros((LANES - GLA_RANK, GLA_QK), F32)], axis=0)
        gla_in, loc, qt3, k, vt3, kmean = _inproj(xs, pos, row(g_mix[l]), _pack_w_in(w_in[l]), wa2,
                                                  row(b_gla_a[l]), invf)
        o_gla = _gla(gla_in, row(jnp.tile(g_gla_norm[l], GLA_HEADS)))
        o_pool, o_sgu = _local(loc, _block_diag(w_pool[l]).astype(BF16), row(pool_scale[l]), row(g_sgu[l]),
                               w_sgu[l], jnp.repeat(b_sgu[l].T, SGU_GW, axis=1))
        o_moba = _moba(qt3, k.reshape(seq // MOBA_BLOCK, MOBA_BLOCK, MOBA_W), vt3,
                       kmean.reshape(seq // MOBA_BLOCK, MOBA_W))
        xs = _merge(xs, row(g_mix[l]), (o_gla, o_pool, o_sgu, o_moba), w_gate[l].astype(BF16),
                    w_branch[l].astype(BF16), w_out[l].astype(BF16))
        xs = _ffn(xs, row(g_ffn[l]), w_up[l].astype(BF16), conv_w[l], row(conv_b[l]), w_down[l].astype(BF16),
                  row(g_final), final_norm=(l == depth - 1))
    return xs[None]
```

```python
import functools

import numpy as np
import jax
import jax.numpy as jnp
from jax import lax
from jax.experimental import pallas as pl
from jax.experimental.pallas import tpu as pltpu

F32 = jnp.float32
BF16 = jnp.bfloat16
HIGHEST = lax.Precision.HIGHEST

D_MODEL = 1024
N_BRANCH = 4
BRANCH_W = 256
GLA_HEADS, GLA_DK, GLA_DV, GLA_RANK, GLA_TAU = 4, 32, 64, 16, 16.0
GLA_QK = GLA_HEADS * GLA_DK
GLA_V = GLA_HEADS * GLA_DV
POOL_WINDOWS = (2, 4, 8, 16)
POOL_GW = BRANCH_W // len(POOL_WINDOWS)
SGU_CHUNK, SGU_GROUPS = 128, 4
SGU_GW = BRANCH_W // SGU_GROUPS
MOBA_HEADS, MOBA_DH, MOBA_BLOCK, MOBA_TOPK = 4, 64, 256, 3
MOBA_W = MOBA_HEADS * MOBA_DH
MOBA_PV_ROWS = MOBA_DH + 16
MOBA_UNROLL = 4
ROPE_THETA = 10000.0
D_FF = 2816
EPS = 1e-6

LANES = 128
VMEM_LIMIT_BYTES = 56 * 1024 * 1024

ROW_TILE = 512
FFN_TILE = 512
GLA_TILE = 256
GLA_SUB = 16
HALO = 16
FF_CHUNK = 256
NEG_BIG = -1e30
LOG2_E = float(np.log2(np.e))

C_GLA = 0
C_GA = C_GLA + 2 * GLA_QK + 2 * GLA_V
C_LOC = C_GA + LANES
C_MQ = C_LOC + BRANCH_W + 2 * BRANCH_W
C_MK = C_MQ + MOBA_W
C_MV = C_MK + MOBA_W
C_END = C_MV + MOBA_W
GLA_IN_W = C_GA + GLA_QK
LOC_W = 3 * BRANCH_W


def _rms(x, g):
    return x * lax.rsqrt(jnp.mean(x * x, axis=-1, keepdims=True) + EPS) * g


def _sigmoid(x):
    return 1.0 / (1.0 + jnp.exp(-x))


def _params(*sem):
    return pltpu.CompilerParams(dimension_semantics=sem, vmem_limit_bytes=VMEM_LIMIT_BYTES)


def _const_spec(shape):
    return pl.BlockSpec(shape, lambda i: (0,) * len(shape), pipeline_mode=pl.Buffered(1))


def _rope_tile(x, cos, sin, first_half):
    partner = jnp.where(first_half, -pltpu.roll(x, LANES - MOBA_DH // 2, axis=1),
                        pltpu.roll(x, MOBA_DH // 2, axis=1))
    return x * cos + partner * sin


def _inproj_kernel(x_ref, pos_ref, g_ref, w_ref, wa2_ref, ba_ref, invf_ref,
                   gla_ref, loc_ref, qt_ref, k_ref, vt_ref, kmean_ref):
    rows = x_ref.shape[0]
    h = _rms(x_ref[...], g_ref[...]).astype(BF16)
    proj = jnp.dot(h, w_ref[...], preferred_element_type=F32)
    proj_m = proj[:, C_MQ:C_END]
    loc_ref[...] = proj[:, C_LOC:C_MQ]

    ang = pos_ref[...].astype(F32) * invf_ref[...]
    cos, sin = jnp.cos(ang), jnp.sin(ang)
    lane = lax.broadcasted_iota(jnp.int32, (rows, LANES), 1)
    first_half = (lane % MOBA_DH) < (MOBA_DH // 2)

    def roped(c0):
        return jnp.concatenate(
            [_rope_tile(proj_m[:, c0 + t * LANES:c0 + (t + 1) * LANES], cos, sin, first_half)
             for t in range(MOBA_W // LANES)], axis=1)

    q = roped(0) * (MOBA_DH ** -0.5)
    k = roped(MOBA_W)
    v = proj_m[:, 2 * MOBA_W:]

    proj_g = proj[:, C_GLA:C_LOC]
    gla_ref[:, 0:C_GA] = proj_g[:, 0:C_GA]
    logit = jnp.dot(proj_g[:, C_GA:C_LOC], wa2_ref[...], precision=HIGHEST,
                    preferred_element_type=F32) + ba_ref[...]
    log_sig = jnp.minimum(logit, 0.0) - jnp.log(1.0 + jnp.exp(-jnp.abs(logit)))
    gla_ref[:, C_GA:GLA_IN_W] = log_sig * (1.0 / GLA_TAU)

    k_ref[...] = k.astype(BF16)
    for r in range(rows // MOBA_BLOCK):
        sl = slice(r * MOBA_BLOCK, (r + 1) * MOBA_BLOCK)
        qt_ref[r] = q[sl].T
        vt_ref[r] = v[sl].T.astype(BF16)
        kmean_ref[0, r:r + 1, :] = jnp.mean(k[sl], axis=0, keepdims=True)


def _inproj(x, pos, g, w_cat, wa2, ba, invf):
    seq = x.shape[0]
    nblk = ROW_TILE // MOBA_BLOCK
    row = lambda w: pl.BlockSpec((ROW_TILE, w), lambda i: (i, 0))
    blk3 = pl.BlockSpec((nblk, MOBA_W, MOBA_BLOCK), lambda i: (i, 0, 0))
    return pl.pallas_call(
        _inproj_kernel,
        grid=(seq // ROW_TILE,),
        in_specs=[row(D_MODEL), row(1), _const_spec((1, D_MODEL)), _const_spec((D_MODEL, C_END)),
                  _const_spec((LANES, GLA_QK)), _const_spec((1, GLA_QK)), _const_spec((1, LANES))],
        out_specs=[row(GLA_IN_W), row(LOC_W), blk3, row(MOBA_W), blk3,
                   pl.BlockSpec((1, nblk, MOBA_W), lambda i: (i, 0, 0))],
        out_shape=[jax.ShapeDtypeStruct((seq, GLA_IN_W), F32),
                   jax.ShapeDtypeStruct((seq, LOC_W), F32),
                   jax.ShapeDtypeStruct((seq // MOBA_BLOCK, MOBA_W, MOBA_BLOCK), F32),
                   jax.ShapeDtypeStruct((seq, MOBA_W), BF16),
                   jax.ShapeDtypeStruct((seq // MOBA_BLOCK, MOBA_W, MOBA_BLOCK), BF16),
                   jax.ShapeDtypeStruct((seq // ROW_TILE, nblk, MOBA_W), F32)],
        compiler_params=_params("parallel"),
        name="inproj",
    )(x, pos, g, w_cat, wa2, ba, invf)


def _gla_kernel(gin_ref, gnorm_ref, out_ref, state_ref, kpad_ref, vpad_ref, bpad_ref,
                qb_ref, ke_ref, eb_ref, o_ref, kv_ref):
    rows = gin_ref.shape[0]

    @pl.when(pl.program_id(0) == 0)
    def _():
        state_ref[...] = jnp.zeros_like(state_ref)
        kpad_ref[0:HALO, :] = jnp.zeros((HALO, GLA_QK), F32)
        vpad_ref[0:HALO, :] = jnp.zeros((HALO, GLA_V), F32)
        bpad_ref[0:HALO, :] = jnp.zeros((HALO, GLA_QK), F32)

    q = gin_ref[:, 0:GLA_QK] * (GLA_DK ** -0.5)
    k = gin_ref[:, GLA_QK:2 * GLA_QK]
    v = gin_ref[:, 2 * GLA_QK:2 * GLA_QK + GLA_V]
    la = gin_ref[:, C_GA:GLA_IN_W]

    rowmod = lax.broadcasted_iota(jnp.int32, (rows, GLA_QK), 0) % GLA_SUB
    b = la
    shift = 1
    while shift < GLA_SUB:
        b = b + jnp.where(rowmod >= shift, pltpu.roll(b, shift, axis=0), 0.0)
        shift *= 2
    b_last = jnp.broadcast_to(b.reshape(rows // GLA_SUB, GLA_SUB, GLA_QK)[:, GLA_SUB - 1:GLA_SUB, :],
                              (rows // GLA_SUB, GLA_SUB, GLA_QK)).reshape(rows, GLA_QK)
    eb = jnp.exp(b)
    qb_ref[...] = (q * eb).astype(BF16)
    ke_ref[...] = (k * jnp.exp(b_last - b)).astype(BF16)
    eb_ref[...] = eb

    kpad_ref[HALO:HALO + rows, :] = k
    vpad_ref[HALO:HALO + rows, :] = v
    bpad_ref[HALO:HALO + rows, :] = b

    head_sum = (lax.broadcasted_iota(jnp.int32, (GLA_QK, GLA_V), 0) // GLA_DK ==
                lax.broadcasted_iota(jnp.int32, (GLA_QK, GLA_V), 1) // GLA_DV).astype(BF16)
    o = jnp.zeros((rows, GLA_V), F32)
    for d in range(GLA_SUB):
        lo = HALO - d
        kd = kpad_ref[lo:lo + rows, :]
        bd = bpad_ref[lo:lo + rows, :]
        vd = vpad_ref[lo:lo + rows, :]
        decay = jnp.exp(jnp.where(rowmod >= d, b - bd, NEG_BIG))
        p = (q * kd * decay).astype(BF16)
        o = o + jnp.dot(p, head_sum, preferred_element_type=F32) * vd
    o_ref[...] = o

    state_mask = (lax.broadcasted_iota(jnp.int32, (GLA_V, GLA_QK), 0) // GLA_DV ==
                  lax.broadcasted_iota(jnp.int32, (GLA_V, GLA_QK), 1) // GLA_DK)

    steps = rows // GLA_SUB
    for s in range(steps):
        blk = slice(s * GLA_SUB, (s + 1) * GLA_SUB)
        v_b = gin_ref[blk, 2 * GLA_QK:2 * GLA_QK + GLA_V].astype(BF16)
        kv = lax.dot_general(v_b, ke_ref[blk, :], (((0,), (0,)), ((), ())),
                             preferred_element_type=F32)
        kv_ref[s] = jnp.where(state_mask, kv, 0.0)

    state = state_ref[...]
    for s in range(steps):
        blk = slice(s * GLA_SUB, (s + 1) * GLA_SUB)
        o_ref[blk, :] += lax.dot_general(qb_ref[blk, :], state.astype(BF16),
                                         (((1,), (1,)), ((), ())), preferred_element_type=F32)
        state = state * eb_ref[(s + 1) * GLA_SUB - 1:(s + 1) * GLA_SUB, :] + kv_ref[s]
    state_ref[...] = state

    o = o_ref[...]
    head_mean = (lax.broadcasted_iota(jnp.int32, (GLA_V, GLA_V), 0) // GLA_DV ==
                 lax.broadcasted_iota(jnp.int32, (GLA_V, GLA_V), 1) // GLA_DV).astype(F32) * (1.0 / GLA_DV)
    ms = jnp.dot(o * o, head_mean, precision=HIGHEST, preferred_element_type=F32)
    r = gin_ref[:, 2 * GLA_QK + GLA_V:C_GA]
    out_ref[...] = o * lax.rsqrt(ms + EPS) * gnorm_ref[...] * (r * _sigmoid(r))


def _gla(gla_in, gnorm):
    seq = gla_in.shape[0]
    return pl.pallas_call(
        _gla_kernel,
        grid=(seq // GLA_TILE,),
        in_specs=[pl.BlockSpec((GLA_TILE, GLA_IN_W), lambda i: (i, 0)), _const_spec((1, GLA_V))],
        out_specs=pl.BlockSpec((GLA_TILE, GLA_V), lambda i: (i, 0)),
        out_shape=jax.ShapeDtypeStruct((seq, GLA_V), F32),
        scratch_shapes=[pltpu.VMEM((GLA_V, GLA_QK), F32),
                        pltpu.VMEM((HALO + GLA_TILE, GLA_QK), F32),
                        pltpu.VMEM((HALO + GLA_TILE, GLA_V), F32),
                        pltpu.VMEM((HALO + GLA_TILE, GLA_QK), F32),
                        pltpu.VMEM((GLA_TILE, GLA_QK), BF16),
                        pltpu.VMEM((GLA_TILE, GLA_QK), BF16),
                        pltpu.VMEM((GLA_TILE, GLA_QK), F32),
                        pltpu.VMEM((GLA_TILE, GLA_V), F32),
                        pltpu.VMEM((GLA_TILE // GLA_SUB, GLA_V, GLA_QK), F32)],
        compiler_params=_params("arbitrary"),
        name="gla",
    )(gla_in, gnorm)


def _local_mixers(loc_ref, halo_ref, wpool_ref, pscale_ref, gsgu_ref, wsgu_ref, bsgu_ref, zpad_ref):
    rows = loc_ref.shape[0]
    i = pl.program_id(0)

    z = loc_ref[:, 0:BRANCH_W]
    zpad_ref[0:HALO, :] = jnp.where(i > 0, halo_ref[...], 0.0)
    zpad_ref[HALO:HALO + rows, :] = z
    t = i * rows + lax.broadcasted_iota(jnp.int32, (rows, BRANCH_W), 0)
    group = lax.broadcasted_iota(jnp.int32, (rows, BRANCH_W), 1) // POOL_GW
    acc = zpad_ref[...]
    mean = jnp.zeros_like(z)
    width = 1
    for g, w in enumerate(POOL_WINDOWS):
        while width < w:
            acc = acc + pltpu.roll(acc, width, axis=0)
            width *= 2
        count = jnp.minimum(t + 1, w).astype(F32)
        mean = jnp.where(group == g, acc[HALO:] / count, mean)
    mixed = (mean - z).astype(BF16)
    pool = jnp.dot(mixed, wpool_ref[...], preferred_element_type=F32) * pscale_ref[...]

    zs = loc_ref[:, BRANCH_W:LOC_W]
    gz = 0.5 * zs * (1.0 + jnp.tanh(np.sqrt(2.0 / np.pi).astype(np.float32) * (zs + 0.044715 * (zs * zs * zs))))
    u = gz[:, 0:BRANCH_W]
    vn = _rms(gz[:, BRANCH_W:], gsgu_ref[...]).astype(BF16)
    tril = (lax.broadcasted_iota(jnp.int32, (SGU_CHUNK, SGU_CHUNK), 0) >=
            lax.broadcasted_iota(jnp.int32, (SGU_CHUNK, SGU_CHUNK), 1))
    w_all = jnp.concatenate([jnp.where(tril, wsgu_ref[g], 0.0).astype(BF16) for g in range(SGU_GROUPS)], axis=1)
    cgroup = lax.broadcasted_iota(jnp.int32, (SGU_CHUNK, BRANCH_W), 1) // SGU_GW
    sgu = []
    for c in range(rows // SGU_CHUNK):
        sl = slice(c * SGU_CHUNK, (c + 1) * SGU_CHUNK)
        v_grp = jnp.concatenate([jnp.where(cgroup == g, vn[sl], 0.0) for g in range(SGU_GROUPS)], axis=0)
        sv = jnp.dot(w_all, v_grp, preferred_element_type=F32) + bsgu_ref[...]
        sgu.append(u[sl] * sv)
    return pool, jnp.concatenate(sgu, axis=0)


def _moba_kernel(qt_ref, k_ref, vt_ref, kmean_ref, o_ref, sel_ref, qtb_ref, m_ref, acc_ref, out_ref, s_ref):
    own = pl.program_id(0)
    nblk = k_ref.shape[0]
    blk_id = lax.broadcasted_iota(jnp.int32, (nblk, MOBA_BLOCK), 0)
    pair_row = lax.broadcasted_iota(jnp.int32, (LANES, MOBA_BLOCK), 0) // MOBA_DH
    key_pos = lax.broadcasted_iota(jnp.int32, (MOBA_BLOCK, MOBA_BLOCK), 0)
    q_pos = lax.broadcasted_iota(jnp.int32, (MOBA_BLOCK, MOBA_BLOCK), 1)
    ones_rows = (lax.broadcasted_iota(jnp.int32, (MOBA_PV_ROWS - MOBA_DH, MOBA_BLOCK), 0) == 0).astype(BF16)

    for h in range(MOBA_HEADS):
        lanes = pl.ds((h // 2) * LANES, LANES)
        qt = jnp.where(pair_row == (h % 2), qt_ref[0, (h // 2) * LANES:(h // 2 + 1) * LANES, :], 0.0)
        qtb_ref[h] = (qt * LOG2_E).astype(BF16)

        gate = jnp.dot(kmean_ref[:, lanes], qt, precision=HIGHEST, preferred_element_type=F32)
        gate = jnp.where(blk_id < own, gate, -jnp.inf)
        sel = jnp.zeros((nblk, MOBA_BLOCK), F32)
        for _ in range(MOBA_TOPK):
            best = jnp.max(gate, axis=0, keepdims=True)
            first = jnp.min(jnp.where(gate == best, blk_id, nblk), axis=0, keepdims=True)
            pick = (blk_id == first) & (best > -jnp.inf)
            sel = jnp.where(pick, 1.0, sel)
            gate = jnp.where(pick, -jnp.inf, gate)
        sel_ref[h] = sel
        m_ref[h] = jnp.full((1, MOBA_BLOCK), NEG_BIG, F32)
        acc_ref[h] = jnp.zeros((MOBA_PV_ROWS, MOBA_BLOCK), F32)

    def scores(j, h, slot):
        lanes = pl.ds((h // 2) * LANES, LANES)
        s_ref[slot * MOBA_HEADS + h] = jnp.dot(k_ref[j, :, lanes], qtb_ref[h],
                                               preferred_element_type=F32).astype(BF16)

    def attend(j, h, slot, selected, causal):
        s = s_ref[slot * MOBA_HEADS + h]
        if causal:
            s = jnp.where(key_pos <= q_pos, s, NEG_BIG)
        m_run = m_ref[h]
        blk_max = jnp.max(jnp.max(s.reshape(8, MOBA_BLOCK // 8, MOBA_BLOCK), axis=0), axis=0, keepdims=True)
        m_new = jnp.maximum(m_run, jnp.where(selected, blk_max.astype(F32), NEG_BIG))
        shift = jnp.where(selected, m_new, -NEG_BIG).astype(BF16)
        p = jnp.exp2(s - shift)
        v_aug = jnp.concatenate([vt_ref[j, pl.ds(h * MOBA_DH, MOBA_DH), :], ones_rows], axis=0)
        pv = jnp.dot(v_aug, p, preferred_element_type=F32)
        acc_ref[h] = acc_ref[h] * jnp.exp2(m_run - m_new) + pv
        m_ref[h] = m_new

    def past(j, slot):
        for pair in range(MOBA_HEADS // 2):
            scores(j + 1, 2 * pair, 1 - slot)
            scores(j + 1, 2 * pair + 1, 1 - slot)
            for h in (2 * pair, 2 * pair + 1):
                attend(j, h, slot, sel_ref[h, pl.ds(j, 1), :] > 0.0, False)

    def past_group(t, carry):
        for u in range(MOBA_UNROLL):
            past(MOBA_UNROLL * t + u, u % 2)
        return carry

    def own_block(slot):
        everyone = jnp.full((1, MOBA_BLOCK), True)
        for h in range(MOBA_HEADS):
            attend(own, h, slot, everyone, True)

    for h in range(MOBA_HEADS):
        scores(0, h, 0)
    lax.fori_loop(0, own // MOBA_UNROLL, past_group, 0)
    done = (own // MOBA_UNROLL) * MOBA_UNROLL
    for u in range(MOBA_UNROLL - 1):

        @pl.when(own - done > u)
        def _():
            past(done + u, u % 2)

    @pl.when(own % 2 == 0)
    def _():
        own_block(0)

    @pl.when(own % 2 == 1)
    def _():
        own_block(1)

    for h in range(MOBA_HEADS):
        out_ref[h * MOBA_DH:(h + 1) * MOBA_DH, :] = acc_ref[h, 0:MOBA_DH, :] / acc_ref[h, MOBA_DH:MOBA_DH + 1, :]
    o_ref[...] = out_ref[...].T


def _moba(qt3, k3, vt3, kmean):
    nblk = k3.shape[0]
    seq = nblk * MOBA_BLOCK
    return pl.pallas_call(
        _moba_kernel,
        grid=(nblk,),
        in_specs=[pl.BlockSpec((1, MOBA_W, MOBA_BLOCK), lambda i: (i, 0, 0)),
                  _const_spec((nblk, MOBA_BLOCK, MOBA_W)), _const_spec((nblk, MOBA_W, MOBA_BLOCK)),
                  _const_spec((nblk, MOBA_W))],
        out_specs=pl.BlockSpec((MOBA_BLOCK, MOBA_W), lambda i: (i, 0)),
        out_shape=jax.ShapeDtypeStruct((seq, MOBA_W), F32),
        scratch_shapes=[pltpu.VMEM((MOBA_HEADS, nblk, MOBA_BLOCK), F32),
                        pltpu.VMEM((MOBA_HEADS, LANES, MOBA_BLOCK), BF16),
                        pltpu.VMEM((MOBA_HEADS, 1, MOBA_BLOCK), F32),
                        pltpu.VMEM((MOBA_HEADS, MOBA_PV_ROWS, MOBA_BLOCK), F32),
                        pltpu.VMEM((MOBA_W, MOBA_BLOCK), F32),
                        pltpu.VMEM((2 * MOBA_HEADS, MOBA_BLOCK, MOBA_BLOCK), BF16)],
        compiler_params=_params("parallel"),
        name="moba",
    )(qt3, k3, vt3, kmean)


def _merge_kernel(x_ref, g_ref, gla_ref, moba_ref, loc_ref, halo_ref, wpool_ref, pscale_ref, gsgu_ref, wsgu_ref,
                  bsgu_ref, wgate_ref, wbr_ref, wout_ref, y_ref, zpad_ref):
    x = x_ref[...]
    h = _rms(x, g_ref[...]).astype(BF16)

    def gated(b, branch):
        gate = _sigmoid(jnp.dot(h, wgate_ref[:, b * D_MODEL:(b + 1) * D_MODEL], preferred_element_type=F32))
        return gate * jnp.dot(branch.astype(BF16), wbr_ref[b], preferred_element_type=F32)

    merged = gated(0, gla_ref[...]) + gated(3, moba_ref[...])
    pool, sgu = _local_mixers(loc_ref, halo_ref, wpool_ref, pscale_ref, gsgu_ref, wsgu_ref, bsgu_ref, zpad_ref)
    merged = merged + gated(1, pool) + gated(2, sgu)
    y_ref[...] = x + jnp.dot(merged.astype(BF16), wout_ref[...], preferred_element_type=F32)


def _merge(x, g, o_gla, o_moba, loc, wpool_bd, pscale, gsgu, wsgu, bsgu_full, wgate, wbr, wout):
    seq = x.shape[0]
    per = ROW_TILE // HALO
    row = lambda w: pl.BlockSpec((ROW_TILE, w), lambda i: (i, 0))
    return pl.pallas_call(
        _merge_kernel,
        grid=(seq // ROW_TILE,),
        in_specs=[row(D_MODEL), _const_spec((1, D_MODEL)), row(BRANCH_W), row(BRANCH_W), row(LOC_W),
                  pl.BlockSpec((HALO, BRANCH_W), lambda i: (jnp.maximum(i * per - 1, 0), 0)),
                  _const_spec((BRANCH_W, BRANCH_W)), _const_spec((1, BRANCH_W)), _const_spec((1, BRANCH_W)),
                  _const_spec((SGU_GROUPS, SGU_CHUNK, SGU_CHUNK)), _const_spec((SGU_CHUNK, BRANCH_W)),
                  _const_spec((D_MODEL, N_BRANCH * D_MODEL)), _const_spec((N_BRANCH, BRANCH_W, D_MODEL)),
                  _const_spec((D_MODEL, D_MODEL))],
        out_specs=row(D_MODEL),
        out_shape=jax.ShapeDtypeStruct((seq, D_MODEL), F32),
        scratch_shapes=[pltpu.VMEM((HALO + ROW_TILE, BRANCH_W), F32)],
        compiler_params=_params("parallel"),
        name="merge",
    )(x, g, o_gla, o_moba, loc, loc, wpool_bd, pscale, gsgu, wsgu, bsgu_full, wgate, wbr, wout)


def _ffn_kernel(x_ref, halo_ref, g_ref, wup_ref, cw_ref, cb_ref, wdown_ref, gfin_ref, y_ref,
                h_ref, acc_ref, up_ref, act_ref, *, final_norm):
    rows = x_ref.shape[0]
    x = x_ref[...]
    h_halo = _rms(halo_ref[...], g_ref[...])
    h_ref[0:HALO, :] = jnp.where(pl.program_id(0) > 0, h_halo, 0.0).astype(BF16)
    h_ref[HALO:HALO + rows, :] = _rms(x, g_ref[...]).astype(BF16)
    n_chunks = D_FF // FF_CHUNK

    def up_proj(c, slot):
        for half in range(2):
            cols = pl.ds(half * D_FF + c * FF_CHUNK, FF_CHUNK)
            up_ref[slot, half] = jnp.dot(h_ref[...], wup_ref[:, cols], preferred_element_type=F32)

    def conv(c, slot, half):
        cols = pl.ds(half * D_FF + c * FF_CHUNK, FF_CHUNK)
        cw = cw_ref[:, cols]
        return (up_ref[slot, half, HALO - 2:HALO - 2 + rows, :] * cw[0:1] +
                up_ref[slot, half, HALO - 1:HALO - 1 + rows, :] * cw[1:2] +
                up_ref[slot, half, HALO:HALO + rows, :] * cw[2:3] + cb_ref[:, cols])

    acc_ref[...] = x
    up_proj(0, 0)
    for c in range(n_chunks + 1):
        if c + 1 < n_chunks:
            up_proj(c + 1, (c + 1) % 2)
        if c >= 1:
            acc_ref[...] += jnp.dot(act_ref[(c - 1) % 2], wdown_ref[(c - 1) * FF_CHUNK:c * FF_CHUNK, :],
                                    preferred_element_type=F32)
        if c < n_chunks:
            a = conv(c, c % 2, 0)
            b = conv(c, c % 2, 1)
            act_ref[c % 2] = (a * _sigmoid(a) * b).astype(BF16)
    y = acc_ref[...]
    y_ref[...] = _rms(y, gfin_ref[...]) if final_norm else y


def _ffn(x, g, wup, cw, cb, wdown, gfin, final_norm):
    seq = x.shape[0]
    per = FFN_TILE // HALO
    row = pl.BlockSpec((FFN_TILE, D_MODEL), lambda i: (i, 0))
    return pl.pallas_call(
        functools.partial(_ffn_kernel, final_norm=final_norm),
        grid=(seq // FFN_TILE,),
        in_specs=[row, pl.BlockSpec((HALO, D_MODEL), lambda i: (jnp.maximum(i * per - 1, 0), 0)),
                  _const_spec((1, D_MODEL)), _const_spec((D_MODEL, 2 * D_FF)), _const_spec((3, 2 * D_FF)),
                  _const_spec((1, 2 * D_FF)), _const_spec((D_FF, D_MODEL)), _const_spec((1, D_MODEL))],
        out_specs=row,
        out_shape=jax.ShapeDtypeStruct((seq, D_MODEL), F32),
        scratch_shapes=[pltpu.VMEM((HALO + FFN_TILE, D_MODEL), BF16), pltpu.VMEM((FFN_TILE, D_MODEL), F32),
                        pltpu.VMEM((2, 2, HALO + FFN_TILE, FF_CHUNK), F32),
                        pltpu.VMEM((2, FFN_TILE, FF_CHUNK), BF16)],
        compiler_params=_params("parallel"),
        name="conv_ffn",
    )(x, x, g, wup, cw, cb, wdown, gfin)


def _pack_w_in(w_in):
    n_front = 2 * GLA_QK + 2 * GLA_V
    pad = jnp.zeros((w_in.shape[0], LANES - GLA_RANK), w_in.dtype)
    return jnp.concatenate([w_in[:, :n_front + GLA_RANK], pad, w_in[:, n_front + GLA_RANK:]], axis=1).astype(BF16)


def _block_diag(w):
    g, a, b = w.shape
    eye = jnp.eye(g, dtype=w.dtype)
    return (eye[:, None, :, None] * w[:, :, None, :]).reshape(g * a, g * b)


def kernel(x, positions, g_mix, w_in, w_gla_a, b_gla_a, g_gla_norm, w_pool, pool_scale, g_sgu, w_sgu, b_sgu,
           w_gate, w_branch, w_out, g_ffn, w_up, conv_w, conv_b, w_down, g_final):
    bsz, seq, d = x.shape
    assert bsz == 1 and d == D_MODEL and seq % FFN_TILE == 0 and seq % ROW_TILE == 0 and seq % GLA_TILE == 0
    depth = w_in.shape[0]
    xs = x[0]
    pos = positions.reshape(seq, 1)
    half = MOBA_DH // 2
    inv_freq = ROPE_THETA ** (-jnp.arange(half, dtype=F32) / half)
    invf = jnp.tile(inv_freq, LANES // half)[None, :]
    row = lambda a: a.reshape(1, -1)

    for l in range(depth):
        wa2 = jnp.concatenate([w_gla_a[l], jnp.zeros((LANES - GLA_RANK, GLA_QK), F32)], axis=0)
        gla_in, loc, qt3, k, vt3, kmean = _inproj(xs, pos, row(g_mix[l]), _pack_w_in(w_in[l]), wa2,
                                                  row(b_gla_a[l]), invf)
        o_gla = _gla(gla_in, row(jnp.tile(g_gla_norm[l], GLA_HEADS)))
        o_moba = _moba(qt3, k.reshape(seq // MOBA_BLOCK, MOBA_BLOCK, MOBA_W), vt3,
                       kmean.reshape(seq // MOBA_BLOCK, MOBA_W))
        xs = _merge(xs, row(g_mix[l]), o_gla, o_moba, loc, _block_diag(w_pool[l]).astype(BF16),
                    row(pool_scale[l]), row(g_sgu[l]), w_sgu[l], jnp.repeat(b_sgu[l].T, SGU_GW, axis=1),
                    w_gate[l].astype(BF16), w_branch[l].astype(BF16), w_out[l].astype(BF16))
        xs = _ffn(xs, row(g_ffn[l]), w_up[l].astype(BF16), conv_w[l], row(conv_b[l]), w_down[l].astype(BF16),
                  row(g_final), final_norm=(l == depth - 1))
    return xs[None]
```

```python
import functools

import numpy as np
import jax
import jax.numpy as jnp
from jax import lax
from jax.experimental import pallas as pl
from jax.experimental.pallas import tpu as pltpu

F32 = jnp.float32
BF16 = jnp.bfloat16
HIGHEST = lax.Precision.HIGHEST

D_MODEL = 1024
N_BRANCH = 4
BRANCH_W = 256
GLA_HEADS, GLA_DK, GLA_DV, GLA_RANK, GLA_TAU = 4, 32, 64, 16, 16.0
GLA_QK = GLA_HEADS * GLA_DK
GLA_V = GLA_HEADS * GLA_DV
POOL_WINDOWS = (2, 4, 8, 16)
POOL_GW = BRANCH_W // len(POOL_WINDOWS)
SGU_CHUNK, SGU_GROUPS = 128, 4
SGU_GW = BRANCH_W // SGU_GROUPS
MOBA_HEADS, MOBA_DH, MOBA_BLOCK, MOBA_TOPK = 4, 64, 256, 3
MOBA_W = MOBA_HEADS * MOBA_DH
MOBA_PV_ROWS = MOBA_DH + 16
MOBA_UNROLL = 4
ROPE_THETA = 10000.0
D_FF = 2816
EPS = 1e-6

LANES = 128
VMEM_LIMIT_BYTES = 56 * 1024 * 1024

ROW_TILE = 512
FFN_TILE = 512
GLA_SUB = 16
HALO = 16
FF_CHUNK = 256
NEG_BIG = -1e30
LOG2_E = float(np.log2(np.e))

C_GLA = 0
C_GA = C_GLA + 2 * GLA_QK + 2 * GLA_V
C_LOC = C_GA + LANES
C_MQ = C_LOC + BRANCH_W + 2 * BRANCH_W
C_MK = C_MQ + MOBA_W
C_MV = C_MK + MOBA_W
C_END = C_MV + MOBA_W
GLA_IN_W = C_GA + GLA_QK
LOC_W = 3 * BRANCH_W


def _rms(x, g):
    return x * lax.rsqrt(jnp.mean(x * x, axis=-1, keepdims=True) + EPS) * g


def _sigmoid(x):
    return 1.0 / (1.0 + jnp.exp(-x))


def _params(*sem):
    return pltpu.CompilerParams(dimension_semantics=sem, vmem_limit_bytes=VMEM_LIMIT_BYTES)


def _const_spec(shape):
    return pl.BlockSpec(shape, lambda i: (0,) * len(shape), pipeline_mode=pl.Buffered(1))


def _rope_tile(x, cos, sin, first_half):
    partner = jnp.where(first_half, -pltpu.roll(x, LANES - MOBA_DH // 2, axis=1),
                        pltpu.roll(x, MOBA_DH // 2, axis=1))
    return x * cos + partner * sin


def _inproj_kernel(x_ref, pos_ref, g_ref, w_ref, wa2_ref, ba_ref, invf_ref,
                   gla_ref, loc_ref, qt_ref, k_ref, vt_ref, kmean_ref):
    rows = x_ref.shape[0]
    h = _rms(x_ref[...], g_ref[...]).astype(BF16)
    proj = jnp.dot(h, w_ref[...], preferred_element_type=F32)
    proj_m = proj[:, C_MQ:C_END]
    loc_ref[...] = proj[:, C_LOC:C_MQ]

    ang = pos_ref[...].astype(F32) * invf_ref[...]
    cos, sin = jnp.cos(ang), jnp.sin(ang)
    lane = lax.broadcasted_iota(jnp.int32, (rows, LANES), 1)
    first_half = (lane % MOBA_DH) < (MOBA_DH // 2)

    def roped(c0):
        return jnp.concatenate(
            [_rope_tile(proj_m[:, c0 + t * LANES:c0 + (t + 1) * LANES], cos, sin, first_half)
             for t in range(MOBA_W // LANES)], axis=1)

    q = roped(0) * (MOBA_DH ** -0.5)
    k = roped(MOBA_W)
    v = proj_m[:, 2 * MOBA_W:]

    proj_g = proj[:, C_GLA:C_LOC]
    gla_ref[:, 0:C_GA] = proj_g[:, 0:C_GA]
    logit = jnp.dot(proj_g[:, C_GA:C_LOC], wa2_ref[...], precision=HIGHEST,
                    preferred_element_type=F32) + ba_ref[...]
    log_sig = jnp.minimum(logit, 0.0) - jnp.log(1.0 + jnp.exp(-jnp.abs(logit)))
    gla_ref[:, C_GA:GLA_IN_W] = log_sig * (1.0 / GLA_TAU)

    k_ref[...] = k.astype(BF16)
    for r in range(rows // MOBA_BLOCK):
        sl = slice(r * MOBA_BLOCK, (r + 1) * MOBA_BLOCK)
        qt_ref[r] = q[sl].T
        vt_ref[r] = v[sl].T.astype(BF16)
        kmean_ref[0, r:r + 1, :] = jnp.mean(k[sl], axis=0, keepdims=True)


def _inproj(x, pos, g, w_cat, wa2, ba, invf):
    seq = x.shape[0]
    nblk = ROW_TILE // MOBA_BLOCK
    row = lambda w: pl.BlockSpec((ROW_TILE, w), lambda i: (i, 0))
    blk3 = pl.BlockSpec((nblk, MOBA_W, MOBA_BLOCK), lambda i: (i, 0, 0))
    return pl.pallas_call(
        _inproj_kernel,
        grid=(seq // ROW_TILE,),
        in_specs=[row(D_MODEL), row(1), _const_spec((1, D_MODEL)), _const_spec((D_MODEL, C_END)),
                  _const_spec((LANES, GLA_QK)), _const_spec((1, GLA_QK)), _const_spec((1, LANES))],
        out_specs=[row(GLA_IN_W), row(LOC_W), blk3, row(MOBA_W), blk3,
                   pl.BlockSpec((1, nblk, MOBA_W), lambda i: (i, 0, 0))],
        out_shape=[jax.ShapeDtypeStruct((seq, GLA_IN_W), F32),
                   jax.ShapeDtypeStruct((seq, LOC_W), F32),
                   jax.ShapeDtypeStruct((seq // MOBA_BLOCK, MOBA_W, MOBA_BLOCK), F32),
                   jax.ShapeDtypeStruct((seq, MOBA_W), BF16),
                   jax.ShapeDtypeStruct((seq // MOBA_BLOCK, MOBA_W, MOBA_BLOCK), BF16),
                   jax.ShapeDtypeStruct((seq // ROW_TILE, nblk, MOBA_W), F32)],
        compiler_params=_params("parallel"),
        name="inproj",
    )(x, pos, g, w_cat, wa2, ba, invf)


class _GlaTile:
    def __init__(self, gin_ref, gnorm_ref, state_ref, kpad_ref, vpad_ref, bpad_ref, qs_ref, qb_ref, ke_ref,
                 eb_ref, o_ref, kv_ref, p_ref):
        self.gin, self.gnorm, self.state_ref = gin_ref, gnorm_ref, state_ref
        self.kpad, self.vpad, self.bpad = kpad_ref, vpad_ref, bpad_ref
        self.qs, self.qb, self.ke, self.eb, self.o, self.kv = qs_ref, qb_ref, ke_ref, eb_ref, o_ref, kv_ref
        self.p = p_ref
        self.rows = gin_ref.shape[0]
        self.steps = self.rows // GLA_SUB
        self.state = None

    def _rowmod(self):
        return lax.broadcasted_iota(jnp.int32, (self.rows, GLA_QK), 0) % GLA_SUB

    def prologue(self):
        rows = self.rows

        @pl.when(pl.program_id(0) == 0)
        def _():
            self.state_ref[...] = jnp.zeros_like(self.state_ref)
            self.kpad[0:HALO, :] = jnp.zeros((HALO, GLA_QK), F32)
            self.vpad[0:HALO, :] = jnp.zeros((HALO, GLA_V), F32)
            self.bpad[0:HALO, :] = jnp.zeros((HALO, GLA_QK), F32)

        q = self.gin[:, 0:GLA_QK] * (GLA_DK ** -0.5)
        k = self.gin[:, GLA_QK:2 * GLA_QK]
        rowmod = self._rowmod()
        b = self.gin[:, C_GA:GLA_IN_W]
        shift = 1
        while shift < GLA_SUB:
            b = b + jnp.where(rowmod >= shift, pltpu.roll(b, shift, axis=0), 0.0)
            shift *= 2
        b_last = jnp.broadcast_to(b.reshape(rows // GLA_SUB, GLA_SUB, GLA_QK)[:, GLA_SUB - 1:GLA_SUB, :],
                                  (rows // GLA_SUB, GLA_SUB, GLA_QK)).reshape(rows, GLA_QK)
        eb = jnp.exp(b)
        self.qs[...] = q
        self.qb[...] = (q * eb).astype(BF16)
        self.ke[...] = (k * jnp.exp(b_last - b)).astype(BF16)
        self.eb[...] = eb
        self.kpad[HALO:HALO + rows, :] = k
        self.vpad[HALO:HALO + rows, :] = self.gin[:, 2 * GLA_QK:2 * GLA_QK + GLA_V]
        self.bpad[HALO:HALO + rows, :] = b
        self.o[...] = jnp.zeros((rows, GLA_V), F32)

    def diag_weights(self, d):
        rows, lo = self.rows, HALO - d
        diff = self.bpad[HALO:HALO + rows, :] - self.bpad[lo:lo + rows, :]
        decay = jnp.exp(jnp.where(self._rowmod() >= d, diff, NEG_BIG))
        self.p[d] = (self.qs[...] * self.kpad[lo:lo + rows, :] * decay).astype(BF16)

    def diag_apply(self, d):
        rows, lo = self.rows, HALO - d
        head_sum = (lax.broadcasted_iota(jnp.int32, (GLA_QK, GLA_V), 0) // GLA_DK ==
                    lax.broadcasted_iota(jnp.int32, (GLA_QK, GLA_V), 1) // GLA_DV).astype(BF16)
        self.o[...] += jnp.dot(self.p[d], head_sum, preferred_element_type=F32) * self.vpad[lo:lo + rows, :]

    def outer_product(self, s):
        state_mask = (lax.broadcasted_iota(jnp.int32, (GLA_V, GLA_QK), 0) // GLA_DV ==
                      lax.broadcasted_iota(jnp.int32, (GLA_V, GLA_QK), 1) // GLA_DK)
        blk = slice(s * GLA_SUB, (s + 1) * GLA_SUB)
        v_b = self.gin[blk, 2 * GLA_QK:2 * GLA_QK + GLA_V].astype(BF16)
        kv = lax.dot_general(v_b, self.ke[blk, :], (((0,), (0,)), ((), ())),
                             preferred_element_type=F32)
        self.kv[s] = jnp.where(state_mask, kv, 0.0)

    def recur(self, s):
        if s == 0:
            self.state = self.state_ref[...]
        blk = slice(s * GLA_SUB, (s + 1) * GLA_SUB)
        self.o[blk, :] += lax.dot_general(self.qb[blk, :], self.state.astype(BF16),
                                          (((1,), (1,)), ((), ())), preferred_element_type=F32)
        self.state = self.state * self.eb[(s + 1) * GLA_SUB - 1:(s + 1) * GLA_SUB, :] + self.kv[s]
        if s == self.steps - 1:
            self.state_ref[...] = self.state

    def finish(self):
        o = self.o[...]
        head_mean = (lax.broadcasted_iota(jnp.int32, (GLA_V, GLA_V), 0) // GLA_DV ==
                     lax.broadcasted_iota(jnp.int32, (GLA_V, GLA_V), 1) // GLA_DV).astype(F32) * (1.0 / GLA_DV)
        ms = jnp.dot(o * o, head_mean, precision=HIGHEST, preferred_element_type=F32)
        r = self.gin[:, 2 * GLA_QK + GLA_V:C_GA]
        return o * lax.rsqrt(ms + EPS) * self.gnorm[...] * (r * _sigmoid(r))


def _gla_scratch(rows):
    return [pltpu.VMEM((GLA_V, GLA_QK), F32),
            pltpu.VMEM((HALO + rows, GLA_QK), F32),
            pltpu.VMEM((HALO + rows, GLA_V), F32),
            pltpu.VMEM((HALO + rows, GLA_QK), F32),
            pltpu.VMEM((rows, GLA_QK), F32),
            pltpu.VMEM((rows, GLA_QK), BF16),
            pltpu.VMEM((rows, GLA_QK), BF16),
            pltpu.VMEM((rows, GLA_QK), F32),
            pltpu.VMEM((rows, GLA_V), F32),
            pltpu.VMEM((rows // GLA_SUB, GLA_V, GLA_QK), F32),
            pltpu.VMEM((GLA_SUB, rows, GLA_QK), BF16)]


def _local_mixers(loc_ref, halo_ref, wpool_ref, pscale_ref, gsgu_ref, wsgu_ref, bsgu_ref, zpad_ref):
    rows = loc_ref.shape[0]
    i = pl.program_id(0)

    z = loc_ref[:, 0:BRANCH_W]
    zpad_ref[0:HALO, :] = jnp.where(i > 0, halo_ref[...], 0.0)
    zpad_ref[HALO:HALO + rows, :] = z
    t = i * rows + lax.broadcasted_iota(jnp.int32, (rows, BRANCH_W), 0)
    group = lax.broadcasted_iota(jnp.int32, (rows, BRANCH_W), 1) // POOL_GW
    acc = zpad_ref[...]
    mean = jnp.zeros_like(z)
    width = 1
    for g, w in enumerate(POOL_WINDOWS):
        while width < w:
            acc = acc + pltpu.roll(acc, width, axis=0)
            width *= 2
        count = jnp.minimum(t + 1, w).astype(F32)
        mean = jnp.where(group == g, acc[HALO:] / count, mean)
    mixed = (mean - z).astype(BF16)
    pool = jnp.dot(mixed, wpool_ref[...], preferred_element_type=F32) * pscale_ref[...]

    zs = loc_ref[:, BRANCH_W:LOC_W]
    gz = 0.5 * zs * (1.0 + jnp.tanh(np.sqrt(2.0 / np.pi).astype(np.float32) * (zs + 0.044715 * (zs * zs * zs))))
    u = gz[:, 0:BRANCH_W]
    vn = _rms(gz[:, BRANCH_W:], gsgu_ref[...]).astype(BF16)
    tril = (lax.broadcasted_iota(jnp.int32, (SGU_CHUNK, SGU_CHUNK), 0) >=
            lax.broadcasted_iota(jnp.int32, (SGU_CHUNK, SGU_CHUNK), 1))
    w_all = jnp.concatenate([jnp.where(tril, wsgu_ref[g], 0.0).astype(BF16) for g in range(SGU_GROUPS)], axis=1)
    cgroup = lax.broadcasted_iota(jnp.int32, (SGU_CHUNK, BRANCH_W), 1) // SGU_GW
    sgu = []
    for c in range(rows // SGU_CHUNK):
        sl = slice(c * SGU_CHUNK, (c + 1) * SGU_CHUNK)
        v_grp = jnp.concatenate([jnp.where(cgroup == g, vn[sl], 0.0) for g in range(SGU_GROUPS)], axis=0)
        sv = jnp.dot(w_all, v_grp, preferred_element_type=F32) + bsgu_ref[...]
        sgu.append(u[sl] * sv)
    return pool, jnp.concatenate(sgu, axis=0)


def _moba_kernel(qt_ref, k_ref, vt_ref, kmean_ref, o_ref, sel_ref, qtb_ref, m_ref, acc_ref, out_ref, s_ref):
    own = pl.program_id(0)
    nblk = k_ref.shape[0]
    blk_id = lax.broadcasted_iota(jnp.int32, (nblk, MOBA_BLOCK), 0)
    pair_row = lax.broadcasted_iota(jnp.int32, (LANES, MOBA_BLOCK), 0) // MOBA_DH
    key_pos = lax.broadcasted_iota(jnp.int32, (MOBA_BLOCK, MOBA_BLOCK), 0)
    q_pos = lax.broadcasted_iota(jnp.int32, (MOBA_BLOCK, MOBA_BLOCK), 1)
    ones_rows = (lax.broadcasted_iota(jnp.int32, (MOBA_PV_ROWS - MOBA_DH, MOBA_BLOCK), 0) == 0).astype(BF16)

    for h in range(MOBA_HEADS):
        lanes = pl.ds((h // 2) * LANES, LANES)
        qt = jnp.where(pair_row == (h % 2), qt_ref[0, (h // 2) * LANES:(h // 2 + 1) * LANES, :], 0.0)
        qtb_ref[h] = (qt * LOG2_E).astype(BF16)

        gate = jnp.dot(kmean_ref[:, lanes], qt, precision=HIGHEST, preferred_element_type=F32)
        gate = jnp.where(blk_id < own, gate, -jnp.inf)
        sel = jnp.zeros((nblk, MOBA_BLOCK), F32)
        for _ in range(MOBA_TOPK):
            best = jnp.max(gate, axis=0, keepdims=True)
            first = jnp.min(jnp.where(gate == best, blk_id, nblk), axis=0, keepdims=True)
            pick = (blk_id == first) & (best > -jnp.inf)
            sel = jnp.where(pick, 1.0, sel)
            gate = jnp.where(pick, -jnp.inf, gate)
        sel_ref[h] = sel
        m_ref[h] = jnp.full((1, MOBA_BLOCK), NEG_BIG, F32)
        acc_ref[h] = jnp.zeros((MOBA_PV_ROWS, MOBA_BLOCK), F32)

    def scores(j, h, slot):
        lanes = pl.ds((h // 2) * LANES, LANES)
        s_ref[slot * MOBA_HEADS + h] = jnp.dot(k_ref[j, :, lanes], qtb_ref[h],
                                               preferred_element_type=F32).astype(BF16)

    def attend(j, h, slot, selected, causal):
        s = s_ref[slot * MOBA_HEADS + h]
        if causal:
            s = jnp.where(key_pos <= q_pos, s, NEG_BIG)
        m_run = m_ref[h]
        blk_max = jnp.max(jnp.max(s.reshape(8, MOBA_BLOCK // 8, MOBA_BLOCK), axis=0), axis=0, keepdims=True)
        m_new = jnp.maximum(m_run, jnp.where(selected, blk_max.astype(F32), NEG_BIG))
        shift = jnp.where(selected, m_new, -NEG_BIG).astype(BF16)
        p = jnp.exp2(s - shift)
        v_aug = jnp.concatenate([vt_ref[j, pl.ds(h * MOBA_DH, MOBA_DH), :], ones_rows], axis=0)
        pv = jnp.dot(v_aug, p, preferred_element_type=F32)
        acc_ref[h] = acc_ref[h] * jnp.exp2(m_run - m_new) + pv
        m_ref[h] = m_new

    def past(j, slot):
        for pair in range(MOBA_HEADS // 2):
            scores(j + 1, 2 * pair, 1 - slot)
            scores(j + 1, 2 * pair + 1, 1 - slot)
            for h in (2 * pair, 2 * pair + 1):
                attend(j, h, slot, sel_ref[h, pl.ds(j, 1), :] > 0.0, False)

    def past_group(t, carry):
        for u in range(MOBA_UNROLL):
            past(MOBA_UNROLL * t + u, u % 2)
        return carry

    def own_block(slot):
        everyone = jnp.full((1, MOBA_BLOCK), True)
        for h in range(MOBA_HEADS):
            attend(own, h, slot, everyone, True)

    for h in range(MOBA_HEADS):
        scores(0, h, 0)
    lax.fori_loop(0, own // MOBA_UNROLL, past_group, 0)
    done = (own // MOBA_UNROLL) * MOBA_UNROLL
    for u in range(MOBA_UNROLL - 1):

        @pl.when(own - done > u)
        def _():
            past(done + u, u % 2)

    @pl.when(own % 2 == 0)
    def _():
        own_block(0)

    @pl.when(own % 2 == 1)
    def _():
        own_block(1)

    for h in range(MOBA_HEADS):
        out_ref[h * MOBA_DH:(h + 1) * MOBA_DH, :] = acc_ref[h, 0:MOBA_DH, :] / acc_ref[h, MOBA_DH:MOBA_DH + 1, :]
    o_ref[...] = out_ref[...].T


def _moba(qt3, k3, vt3, kmean):
    nblk = k3.shape[0]
    seq = nblk * MOBA_BLOCK
    return pl.pallas_call(
        _moba_kernel,
        grid=(nblk,),
        in_specs=[pl.BlockSpec((1, MOBA_W, MOBA_BLOCK), lambda i: (i, 0, 0)),
                  _const_spec((nblk, MOBA_BLOCK, MOBA_W)), _const_spec((nblk, MOBA_W, MOBA_BLOCK)),
                  _const_spec((nblk, MOBA_W))],
        out_specs=pl.BlockSpec((MOBA_BLOCK, MOBA_W), lambda i: (i, 0)),
        out_shape=jax.ShapeDtypeStruct((seq, MOBA_W), F32),
        scratch_shapes=[pltpu.VMEM((MOBA_HEADS, nblk, MOBA_BLOCK), F32),
                        pltpu.VMEM((MOBA_HEADS, LANES, MOBA_BLOCK), BF16),
                        pltpu.VMEM((MOBA_HEADS, 1, MOBA_BLOCK), F32),
                        pltpu.VMEM((MOBA_HEADS, MOBA_PV_ROWS, MOBA_BLOCK), F32),
                        pltpu.VMEM((MOBA_W, MOBA_BLOCK), F32),
                        pltpu.VMEM((2 * MOBA_HEADS, MOBA_BLOCK, MOBA_BLOCK), BF16)],
        compiler_params=_params("parallel"),
        name="moba",
    )(qt3, k3, vt3, kmean)


def _merge_kernel(x_ref, g_ref, gin_ref, gnorm_ref, moba_ref, loc_ref, halo_ref, wpool_ref, pscale_ref, gsgu_ref,
                  wsgu_ref, bsgu_ref, wgate_ref, wbr_ref, wout_ref, y_ref, zpad_ref, *gla_scratch):
    x = x_ref[...]
    h = _rms(x, g_ref[...]).astype(BF16)
    gla = _GlaTile(gin_ref, gnorm_ref, *gla_scratch)
    half = D_MODEL // 2

    def gate_half(b, part):
        cols = slice(b * D_MODEL + part * half, b * D_MODEL + (part + 1) * half)
        return _sigmoid(jnp.dot(h, wgate_ref[:, cols], preferred_element_type=F32))

    def branch(b, value):
        return jnp.dot(value.astype(BF16), wbr_ref[b], preferred_element_type=F32)

    quarter = GLA_SUB // 4
    steps = gla.steps
    diag_groups = [range(c * quarter, (c + 1) * quarter) for c in range(4)]
    step_groups = [range(c * steps // 2, (c + 1) * steps // 2) for c in range(2)]
    after_gate = ([lambda c=c: ([gla.diag_apply(d) for d in diag_groups[c]],
                                [gla.diag_weights(d) for d in (diag_groups[c + 1] if c < 3 else ())])
                   for c in range(4)] +
                  [lambda c=c: [gla.outer_product(s) for s in step_groups[c]] for c in range(2)] +
                  [lambda c=c: [gla.recur(s) for s in step_groups[c]] for c in range(2)])
    gla.prologue()
    for d in diag_groups[0]:
        gla.diag_weights(d)
    gates = []
    for b in range(N_BRANCH):
        parts = []
        for part in range(2):
            parts.append(gate_half(b, part))
            after_gate[2 * b + part]()
        gates.append(jnp.concatenate(parts, axis=1))
    o_gla = gla.finish()

    merged = gates[3] * branch(3, moba_ref[...]) + gates[0] * branch(0, o_gla)
    pool, sgu = _local_mixers(loc_ref, halo_ref, wpool_ref, pscale_ref, gsgu_ref, wsgu_ref, bsgu_ref, zpad_ref)
    merged = merged + gates[1] * branch(1, pool) + gates[2] * branch(2, sgu)
    y_ref[...] = x + jnp.dot(merged.astype(BF16), wout_ref[...], preferred_element_type=F32)


def _merge(x, g, gla_in, gnorm, o_moba, loc, wpool_bd, pscale, gsgu, wsgu, bsgu_full, wgate, wbr, wout):
    seq = x.shape[0]
    per = ROW_TILE // HALO
    row = lambda w: pl.BlockSpec((ROW_TILE, w), lambda i: (i, 0))
    return pl.pallas_call(
        _merge_kernel,
        grid=(seq // ROW_TILE,),
        in_specs=[row(D_MODEL), _const_spec((1, D_MODEL)), row(GLA_IN_W), _const_spec((1, GLA_V)), row(BRANCH_W),
                  row(LOC_W), pl.BlockSpec((HALO, BRANCH_W), lambda i: (jnp.maximum(i * per - 1, 0), 0)),
                  _const_spec((BRANCH_W, BRANCH_W)), _const_spec((1, BRANCH_W)), _const_spec((1, BRANCH_W)),
                  _const_spec((SGU_GROUPS, SGU_CHUNK, SGU_CHUNK)), _const_spec((SGU_CHUNK, BRANCH_W)),
                  _const_spec((D_MODEL, N_BRANCH * D_MODEL)), _const_spec((N_BRANCH, BRANCH_W, D_MODEL)),
                  _const_spec((D_MODEL, D_MODEL))],
        out_specs=row(D_MODEL),
        out_shape=jax.ShapeDtypeStruct((seq, D_MODEL), F32),
        scratch_shapes=[pltpu.VMEM((HALO + ROW_TILE, BRANCH_W), F32)] + _gla_scratch(ROW_TILE),
        compiler_params=_params("arbitrary"),
        name="merge",
    )(x, g, gla_in, gnorm, o_moba, loc, loc, wpool_bd, pscale, gsgu, wsgu, bsgu_full, wgate, wbr, wout)


def _ffn_kernel(x_ref, halo_ref, g_ref, wup_ref, cw_ref, cb_ref, wdown_ref, gfin_ref, y_ref,
                h_ref, acc_ref, up_ref, act_ref, *, final_norm):
    rows = x_ref.shape[0]
    x = x_ref[...]
    h_halo = _rms(halo_ref[...], g_ref[...])
    h_ref[0:HALO, :] = jnp.where(pl.program_id(0) > 0, h_halo, 0.0).astype(BF16)
    h_ref[HALO:HALO + rows, :] = _rms(x, g_ref[...]).astype(BF16)
    n_chunks = D_FF // FF_CHUNK

    def up_proj(c, slot):
        for half in range(2):
            cols = pl.ds(half * D_FF + c * FF_CHUNK, FF_CHUNK)
            up_ref[slot, half] = jnp.dot(h_ref[...], wup_ref[:, cols], preferred_element_type=F32)

    def conv(c, slot, half):
        cols = pl.ds(half * D_FF + c * FF_CHUNK, FF_CHUNK)
        cw = cw_ref[:, cols]
        return (up_ref[slot, half, HALO - 2:HALO - 2 + rows, :] * cw[0:1] +
                up_ref[slot, half, HALO - 1:HALO - 1 + rows, :] * cw[1:2] +
                up_ref[slot, half, HALO:HALO + rows, :] * cw[2:3] + cb_ref[:, cols])

    acc_ref[...] = x
    up_proj(0, 0)
    for c in range(n_chunks + 1):
        if c + 1 < n_chunks:
            up_proj(c + 1, (c + 1) % 2)
        if c >= 1:
            acc_ref[...] += jnp.dot(act_ref[(c - 1) % 2], wdown_ref[(c - 1) * FF_CHUNK:c * FF_CHUNK, :],
                                    preferred_element_type=F32)
        if c < n_chunks:
            a = conv(c, c % 2, 0)
            b = conv(c, c % 2, 1)
            act_ref[c % 2] = (a * _sigmoid(a) * b).astype(BF16)
    y = acc_ref[...]
    y_ref[...] = _rms(y, gfin_ref[...]) if final_norm else y


def _ffn(x, g, wup, cw, cb, wdown, gfin, final_norm):
    seq = x.shape[0]
    per = FFN_TILE // HALO
    row = pl.BlockSpec((FFN_TILE, D_MODEL), lambda i: (i, 0))
    return pl.pallas_call(
        functools.partial(_ffn_kernel, final_norm=final_norm),
        grid=(seq // FFN_TILE,),
        in_specs=[row, pl.BlockSpec((HALO, D_MODEL), lambda i: (jnp.maximum(i * per - 1, 0), 0)),
                  _const_spec((1, D_MODEL)), _const_spec((D_MODEL, 2 * D_FF)), _const_spec((3, 2 * D_FF)),
                  _const_spec((1, 2 * D_FF)), _const_spec((D_FF, D_MODEL)), _const_spec((1, D_MODEL))],
        out_specs=row,
        out_shape=jax.ShapeDtypeStruct((seq, D_MODEL), F32),
        scratch_shapes=[pltpu.VMEM((HALO + FFN_TILE, D_MODEL), BF16), pltpu.VMEM((FFN_TILE, D_MODEL), F32),
                        pltpu.VMEM((2, 2, HALO + FFN_TILE, FF_CHUNK), F32),
                        pltpu.VMEM((2, FFN_TILE, FF_CHUNK), BF16)],
        compiler_params=_params("parallel"),
        name="conv_ffn",
    )(x, x, g, wup, cw, cb, wdown, gfin)


def _pack_w_in(w_in):
    n_front = 2 * GLA_QK + 2 * GLA_V
    pad = jnp.zeros((w_in.shape[0], LANES - GLA_RANK), w_in.dtype)
    return jnp.concatenate([w_in[:, :n_front + GLA_RANK], pad, w_in[:, n_front + GLA_RANK:]], axis=1).astype(BF16)


def _block_diag(w):
    g, a, b = w.shape
    eye = jnp.eye(g, dtype=w.dtype)
    return (eye[:, None, :, None] * w[:, :, None, :]).reshape(g * a, g * b)


def kernel(x, positions, g_mix, w_in, w_gla_a, b_gla_a, g_gla_norm, w_pool, pool_scale, g_sgu, w_sgu, b_sgu,
           w_gate, w_branch, w_out, g_ffn, w_up, conv_w, conv_b, w_down, g_final):
    bsz, seq, d = x.shape
    assert bsz == 1 and d == D_MODEL and seq % FFN_TILE == 0 and seq % ROW_TILE == 0
    depth = w_in.shape[0]
    xs = x[0]
    pos = positions.reshape(seq, 1)
    half = MOBA_DH // 2
    inv_freq = ROPE_THETA ** (-jnp.arange(half, dtype=F32) / half)
    invf = jnp.tile(inv_freq, LANES // half)[None, :]
    row = lambda a: a.reshape(1, -1)

    for l in range(depth):
        wa2 = jnp.concatenate([w_gla_a[l], jnp.zeros((LANES - GLA_RANK, GLA_QK), F32)], axis=0)
        gla_in, loc, qt3, k, vt3, kmean = _inproj(xs, pos, row(g_mix[l]), _pack_w_in(w_in[l]), wa2,
                                                  row(b_gla_a[l]), invf)
        o_moba = _moba(qt3, k.reshape(seq // MOBA_BLOCK, MOBA_BLOCK, MOBA_W), vt3,
                       kmean.reshape(seq // MOBA_BLOCK, MOBA_W))
        xs = _merge(xs, row(g_mix[l]), gla_in, row(jnp.tile(g_gla_norm[l], GLA_HEADS)), o_moba, loc,
                    _block_diag(w_pool[l]).astype(BF16),
                    row(pool_scale[l]), row(g_sgu[l]), w_sgu[l], jnp.repeat(b_sgu[l].T, SGU_GW, axis=1),
                    w_gate[l].astype(BF16), w_branch[l].astype(BF16), w_out[l].astype(BF16))
        xs = _ffn(xs, row(g_ffn[l]), w_up[l].astype(BF16), conv_w[l], row(conv_b[l]), w_down[l].astype(BF16),
                  row(g_final), final_norm=(l == depth - 1))
    return xs[None]
```

```python
import functools

import numpy as np
import jax
import jax.numpy as jnp
from jax import lax
from jax.experimental import pallas as pl
from jax.experimental.pallas import tpu as pltpu

F32 = jnp.float32
BF16 = jnp.bfloat16
HIGHEST = lax.Precision.HIGHEST

D_MODEL = 1024
N_BRANCH = 4
BRANCH_W = 256
GLA_HEADS, GLA_DK, GLA_DV, GLA_RANK, GLA_TAU = 4, 32, 64, 16, 16.0
GLA_QK = GLA_HEADS * GLA_DK
GLA_V = GLA_HEADS * GLA_DV
POOL_WINDOWS = (2, 4, 8, 16)
POOL_GW = BRANCH_W // len(POOL_WINDOWS)
SGU_CHUNK, SGU_GROUPS = 128, 4
SGU_GW = BRANCH_W // SGU_GROUPS
MOBA_HEADS, MOBA_DH, MOBA_BLOCK, MOBA_TOPK = 4, 64, 256, 3
MOBA_W = MOBA_HEADS * MOBA_DH
MOBA_PV_ROWS = MOBA_DH + 16
MOBA_UNROLL = 4
ROPE_THETA = 10000.0
D_FF = 2816
EPS = 1e-6

LANES = 128
VMEM_LIMIT_BYTES = 56 * 1024 * 1024

ROW_TILE = 512
FFN_TILE = 512
GLA_TILE = 256
GLA_SUB = 16
HALO = 16
FF_CHUNK = 256
CAST_ROWS = 256
NEG_BIG = -1e30
LOG2_E = float(np.log2(np.e))

C_GLA = 0
C_GA = C_GLA + 2 * GLA_QK + 2 * GLA_V
C_LOC = C_GA + LANES
C_MQ = C_LOC + BRANCH_W + 2 * BRANCH_W
C_MK = C_MQ + MOBA_W
C_MV = C_MK + MOBA_W
C_END = C_MV + MOBA_W
GLA_IN_W = C_GA + GLA_QK
LOC_W = 3 * BRANCH_W


def _rms(x, g):
    return x * lax.rsqrt(jnp.mean(x * x, axis=-1, keepdims=True) + EPS) * g


def _sigmoid(x):
    return 1.0 / (1.0 + jnp.exp(-x))


def _params(*sem):
    return pltpu.CompilerParams(dimension_semantics=sem, vmem_limit_bytes=VMEM_LIMIT_BYTES)


def _const_spec(shape):
    return pl.BlockSpec(shape, lambda i: (0,) * len(shape), pipeline_mode=pl.Buffered(1))


def _rope_tile(x, cos, sin, first_half):
    partner = jnp.where(first_half, -pltpu.roll(x, LANES - MOBA_DH // 2, axis=1),
                        pltpu.roll(x, MOBA_DH // 2, axis=1))
    return x * cos + partner * sin


def _inproj_kernel(x_ref, pos_ref, g_ref, w_ref, wa2_ref, ba_ref, invf_ref,
                   gla_ref, loc_ref, qt_ref, k_ref, vt_ref, kmean_ref):
    rows = x_ref.shape[0]
    lane = lax.broadcasted_iota(jnp.int32, (MOBA_BLOCK, LANES), 1)
    first_half = (lane % MOBA_DH) < (MOBA_DH // 2)

    projs = []
    for r in range(rows // MOBA_BLOCK):
        sl = slice(r * MOBA_BLOCK, (r + 1) * MOBA_BLOCK)
        h = _rms(x_ref[sl, :], g_ref[...]).astype(BF16)
        projs.append(jnp.dot(h, w_ref[...], preferred_element_type=F32))

    for r, proj in enumerate(projs):
        sl = slice(r * MOBA_BLOCK, (r + 1) * MOBA_BLOCK)
        proj_m = proj[:, C_MQ:C_END]
        loc_ref[sl, :] = proj[:, C_LOC:C_MQ]

        ang = pos_ref[sl, :].astype(F32) * invf_ref[...]
        cos, sin = jnp.cos(ang), jnp.sin(ang)

        def roped(c0):
            return jnp.concatenate(
                [_rope_tile(proj_m[:, c0 + t * LANES:c0 + (t + 1) * LANES], cos, sin, first_half)
                 for t in range(MOBA_W // LANES)], axis=1)

        q = roped(0) * (MOBA_DH ** -0.5)
        k = roped(MOBA_W)
        v = proj_m[:, 2 * MOBA_W:]

        proj_g = proj[:, C_GLA:C_LOC]
        gla_ref[sl, 0:C_GA] = proj_g[:, 0:C_GA]
        logit = jnp.dot(proj_g[:, C_GA:C_LOC], wa2_ref[...], precision=HIGHEST,
                        preferred_element_type=F32) + ba_ref[...]
        log_sig = jnp.minimum(logit, 0.0) - jnp.log(1.0 + jnp.exp(-jnp.abs(logit)))
        gla_ref[sl, C_GA:GLA_IN_W] = log_sig * (1.0 / GLA_TAU)

        k_ref[sl, :] = k.astype(BF16)
        qt_ref[r] = q.T
        vt_ref[r] = v.T.astype(BF16)
        kmean_ref[0, r:r + 1, :] = jnp.mean(k, axis=0, keepdims=True)


def _inproj(x, pos, g, w_cat, wa2, ba, invf):
    seq = x.shape[0]
    nblk = ROW_TILE // MOBA_BLOCK
    row = lambda w: pl.BlockSpec((ROW_TILE, w), lambda i: (i, 0))
    blk3 = pl.BlockSpec((nblk, MOBA_W, MOBA_BLOCK), lambda i: (i, 0, 0))
    return pl.pallas_call(
        _inproj_kernel,
        grid=(seq // ROW_TILE,),
        in_specs=[row(D_MODEL), row(1), _const_spec((1, D_MODEL)), _const_spec((D_MODEL, C_END)),
                  _const_spec((LANES, GLA_QK)), _const_spec((1, GLA_QK)), _const_spec((1, LANES))],
        out_specs=[row(GLA_IN_W), row(LOC_W), blk3, row(MOBA_W), blk3,
                   pl.BlockSpec((1, nblk, MOBA_W), lambda i: (i, 0, 0))],
        out_shape=[jax.ShapeDtypeStruct((seq, GLA_IN_W), F32),
                   jax.ShapeDtypeStruct((seq, LOC_W), F32),
                   jax.ShapeDtypeStruct((seq // MOBA_BLOCK, MOBA_W, MOBA_BLOCK), F32),
                   jax.ShapeDtypeStruct((seq, MOBA_W), BF16),
                   jax.ShapeDtypeStruct((seq // MOBA_BLOCK, MOBA_W, MOBA_BLOCK), BF16),
                   jax.ShapeDtypeStruct((seq // ROW_TILE, nblk, MOBA_W), F32)],
        compiler_params=_params("parallel"),
        name="inproj",
    )(x, pos, g, w_cat, wa2, ba, invf)


def _gla_kernel(gin_ref, gnorm_ref, out_ref, state_ref, kpad_ref, vpad_ref, bpad_ref,
                qb_ref, ke_ref, eb_ref, o_ref, kv_ref):
    rows = gin_ref.shape[0]

    @pl.when(pl.program_id(0) == 0)
    def _():
        state_ref[...] = jnp.zeros_like(state_ref)
        kpad_ref[0:HALO, :] = jnp.zeros((HALO, GLA_QK), F32)
        vpad_ref[0:HALO, :] = jnp.zeros((HALO, GLA_V), F32)
        bpad_ref[0:HALO, :] = jnp.zeros((HALO, GLA_QK), F32)

    q = gin_ref[:, 0:GLA_QK] * (GLA_DK ** -0.5)
    k = gin_ref[:, GLA_QK:2 * GLA_QK]
    v = gin_ref[:, 2 * GLA_QK:2 * GLA_QK + GLA_V]
    la = gin_ref[:, C_GA:GLA_IN_W]

    rowmod = lax.broadcasted_iota(jnp.int32, (rows, GLA_QK), 0) % GLA_SUB
    b = la
    shift = 1
    while shift < GLA_SUB:
        b = b + jnp.where(rowmod >= shift, pltpu.roll(b, shift, axis=0), 0.0)
        shift *= 2
    b_last = jnp.broadcast_to(b.reshape(rows // GLA_SUB, GLA_SUB, GLA_QK)[:, GLA_SUB - 1:GLA_SUB, :],
                              (rows // GLA_SUB, GLA_SUB, GLA_QK)).reshape(rows, GLA_QK)
    eb = jnp.exp(b)
    qb_ref[...] = (q * eb).astype(BF16)
    ke_ref[...] = (k * jnp.exp(b_last - b)).astype(BF16)
    eb_ref[...] = eb

    kpad_ref[HALO:HALO + rows, :] = k
    vpad_ref[HALO:HALO + rows, :] = v
    bpad_ref[HALO:HALO + rows, :] = b

    head_sum = (lax.broadcasted_iota(jnp.int32, (GLA_QK, GLA_V), 0) // GLA_DK ==
                lax.broadcasted_iota(jnp.int32, (GLA_QK, GLA_V), 1) // GLA_DV).astype(BF16)
    o = jnp.zeros((rows, GLA_V), F32)
    for d in range(GLA_SUB):
        lo = HALO - d
        kd = kpad_ref[lo:lo + rows, :]
        bd = bpad_ref[lo:lo + rows, :]
        vd = vpad_ref[lo:lo + rows, :]
        decay = jnp.exp(jnp.where(rowmod >= d, b - bd, NEG_BIG))
        p = (q * kd * decay).astype(BF16)
        o = o + jnp.dot(p, head_sum, preferred_element_type=F32) * vd
    o_ref[...] = o

    state_mask = (lax.broadcasted_iota(jnp.int32, (GLA_V, GLA_QK), 0) // GLA_DV ==
                  lax.broadcasted_iota(jnp.int32, (GLA_V, GLA_QK), 1) // GLA_DK)

    steps = rows // GLA_SUB
    for s in range(steps):
        blk = slice(s * GLA_SUB, (s + 1) * GLA_SUB)
        v_b = gin_ref[blk, 2 * GLA_QK:2 * GLA_QK + GLA_V].astype(BF16)
        kv = lax.dot_general(v_b, ke_ref[blk, :], (((0,), (0,)), ((), ())),
                             preferred_element_type=F32)
        kv_ref[s] = jnp.where(state_mask, kv, 0.0)

    state = state_ref[...]
    for s in range(steps):
        blk = slice(s * GLA_SUB, (s + 1) * GLA_SUB)
        o_ref[blk, :] += lax.dot_general(qb_ref[blk, :], state.astype(BF16),
                                         (((1,), (1,)), ((), ())), preferred_element_type=F32)
        state = state * eb_ref[(s + 1) * GLA_SUB - 1:(s + 1) * GLA_SUB, :] + kv_ref[s]
    state_ref[...] = state

    o = o_ref[...]
    head_mean = (lax.broadcasted_iota(jnp.int32, (GLA_V, GLA_V), 0) // GLA_DV ==
                 lax.broadcasted_iota(jnp.int32, (GLA_V, GLA_V), 1) // GLA_DV).astype(F32) * (1.0 / GLA_DV)
    ms = jnp.dot(o * o, head_mean, precision=HIGHEST, preferred_element_type=F32)
    r = gin_ref[:, 2 * GLA_QK + GLA_V:C_GA]
    out_ref[...] = o * lax.rsqrt(ms + EPS) * gnorm_ref[...] * (r * _sigmoid(r))


def _gla(gla_in, gnorm):
    seq = gla_in.shape[0]
    return pl.pallas_call(
        _gla_kernel,
        grid=(seq // GLA_TILE,),
        in_specs=[pl.BlockSpec((GLA_TILE, GLA_IN_W), lambda i: (i, 0)), _const_spec((1, GLA_V))],
        out_specs=pl.BlockSpec((GLA_TILE, GLA_V), lambda i: (i, 0)),
        out_shape=jax.ShapeDtypeStruct((seq, GLA_V), F32),
        scratch_shapes=[pltpu.VMEM((GLA_V, GLA_QK), F32),
                        pltpu.VMEM((HALO + GLA_TILE, GLA_QK), F32),
                        pltpu.VMEM((HALO + GLA_TILE, GLA_V), F32),
                        pltpu.VMEM((HALO + GLA_TILE, GLA_QK), F32),
                        pltpu.VMEM((GLA_TILE, GLA_QK), BF16),
                        pltpu.VMEM((GLA_TILE, GLA_QK), BF16),
                        pltpu.VMEM((GLA_TILE, GLA_QK), F32),
                        pltpu.VMEM((GLA_TILE, GLA_V), F32),
                        pltpu.VMEM((GLA_TILE // GLA_SUB, GLA_V, GLA_QK), F32)],
        compiler_params=_params("arbitrary"),
        name="gla",
    )(gla_in, gnorm)


def _local_mixers(loc_ref, halo_ref, wpool_ref, pscale_ref, gsgu_ref, wsgu_ref, bsgu_ref, zpad_ref):
    rows = loc_ref.shape[0]
    i = pl.program_id(0)

    z = loc_ref[:, 0:BRANCH_W]
    zpad_ref[0:HALO, :] = jnp.where(i > 0, halo_ref[...], 0.0)
    zpad_ref[HALO:HALO + rows, :] = z
    t = i * rows + lax.broadcasted_iota(jnp.int32, (rows, BRANCH_W), 0)
    group = lax.broadcasted_iota(jnp.int32, (rows, BRANCH_W), 1) // POOL_GW
    acc = zpad_ref[...]
    mean = jnp.zeros_like(z)
    width = 1
    for g, w in enumerate(POOL_WINDOWS):
        while width < w:
            acc = acc + pltpu.roll(acc, width, axis=0)
            width *= 2
        count = jnp.minimum(t + 1, w).astype(F32)
        mean = jnp.where(group == g, acc[HALO:] / count, mean)
    mixed = (mean - z).astype(BF16)
    pool = jnp.dot(mixed, wpool_ref[...], preferred_element_type=F32) * pscale_ref[...]

    zs = loc_ref[:, BRANCH_W:LOC_W]
    gz = 0.5 * zs * (1.0 + jnp.tanh(np.sqrt(2.0 / np.pi).astype(np.float32) * (zs + 0.044715 * (zs * zs * zs))))
    u = gz[:, 0:BRANCH_W]
    vn = _rms(gz[:, BRANCH_W:], gsgu_ref[...]).astype(BF16)
    tril = (lax.broadcasted_iota(jnp.int32, (SGU_CHUNK, SGU_CHUNK), 0) >=
            lax.broadcasted_iota(jnp.int32, (SGU_CHUNK, SGU_CHUNK), 1))
    w_all = jnp.concatenate([jnp.where(tril, wsgu_ref[g], 0.0).astype(BF16) for g in range(SGU_GROUPS)], axis=1)
    cgroup = lax.broadcasted_iota(jnp.int32, (SGU_CHUNK, BRANCH_W), 1) // SGU_GW
    sgu = []
    for c in range(rows // SGU_CHUNK):
        sl = slice(c * SGU_CHUNK, (c + 1) * SGU_CHUNK)
        v_grp = jnp.concatenate([jnp.where(cgroup == g, vn[sl], 0.0) for g in range(SGU_GROUPS)], axis=0)
        sv = jnp.dot(w_all, v_grp, preferred_element_type=F32) + bsgu_ref[...]
        sgu.append(u[sl] * sv)
    return pool, jnp.concatenate(sgu, axis=0)


def _moba_kernel(qt_ref, k_ref, vt_ref, kmean_ref, o_ref, sel_ref, qtb_ref, m_ref, acc_ref, out_ref, s_ref):
    own = pl.program_id(0)
    nblk = k_ref.shape[0]
    blk_id = lax.broadcasted_iota(jnp.int32, (nblk, MOBA_BLOCK), 0)
    pair_row = lax.broadcasted_iota(jnp.int32, (LANES, MOBA_BLOCK), 0) // MOBA_DH
    key_pos = lax.broadcasted_iota(jnp.int32, (MOBA_BLOCK, MOBA_BLOCK), 0)
    q_pos = lax.broadcasted_iota(jnp.int32, (MOBA_BLOCK, MOBA_BLOCK), 1)
    ones_rows = (lax.broadcasted_iota(jnp.int32, (MOBA_PV_ROWS - MOBA_DH, MOBA_BLOCK), 0) == 0).astype(BF16)

    for h in range(MOBA_HEADS):
        lanes = pl.ds((h // 2) * LANES, LANES)
        qt = jnp.where(pair_row == (h % 2), qt_ref[0, (h // 2) * LANES:(h // 2 + 1) * LANES, :], 0.0)
        qtb_ref[h] = (qt * LOG2_E).astype(BF16)

        gate = jnp.dot(kmean_ref[:, lanes], qt, precision=HIGHEST, preferred_element_type=F32)
        gate = jnp.where(blk_id < own, gate, -jnp.inf)
        sel = jnp.zeros((nblk, MOBA_BLOCK), F32)
        for _ in range(MOBA_TOPK):
            best = jnp.max(gate, axis=0, keepdims=True)
            first = jnp.min(jnp.where(gate == best, blk_id, nblk), axis=0, keepdims=True)
            pick = (blk_id == first) & (best > -jnp.inf)
            sel = jnp.where(pick, 1.0, sel)
            gate = jnp.where(pick, -jnp.inf, gate)
        sel_ref[h] = sel
        m_ref[h] = jnp.full((1, MOBA_BLOCK), NEG_BIG, F32)
        acc_ref[h] = jnp.zeros((MOBA_PV_ROWS, MOBA_BLOCK), F32)

    def scores(j, h, slot):
        lanes = pl.ds((h // 2) * LANES, LANES)
        s_ref[slot * MOBA_HEADS + h] = jnp.dot(k_ref[j, :, lanes], qtb_ref[h],
                                               preferred_element_type=F32).astype(BF16)

    def attend(j, h, slot, selected, causal):
        s = s_ref[slot * MOBA_HEADS + h]
        if causal:
            s = jnp.where(key_pos <= q_pos, s, NEG_BIG)
        m_run = m_ref[h]
        blk_max = jnp.max(jnp.max(s.reshape(8, MOBA_BLOCK // 8, MOBA_BLOCK), axis=0), axis=0, keepdims=True)
        m_new = jnp.maximum(m_run, jnp.where(selected, blk_max.astype(F32), NEG_BIG))
        shift = jnp.where(selected, m_new, -NEG_BIG).astype(BF16)
        p = jnp.exp2(s - shift)
        v_aug = jnp.concatenate([vt_ref[j, pl.ds(h * MOBA_DH, MOBA_DH), :], ones_rows], axis=0)
        pv = jnp.dot(v_aug, p, preferred_element_type=F32)
        acc_ref[h] = acc_ref[h] * jnp.exp2(m_run - m_new) + pv
        m_ref[h] = m_new

    def past(j, slot):
        for pair in range(MOBA_HEADS // 2):
            scores(j + 1, 2 * pair, 1 - slot)
            scores(j + 1, 2 * pair + 1, 1 - slot)
            for h in (2 * pair, 2 * pair + 1):
                attend(j, h, slot, sel_ref[h, pl.ds(j, 1), :] > 0.0, False)

    def past_group(t, carry):
        for u in range(MOBA_UNROLL):
            past(MOBA_UNROLL * t + u, u % 2)
        return carry

    def own_block(slot):
        everyone = jnp.full((1, MOBA_BLOCK), True)
        for h in range(MOBA_HEADS):
            attend(own, h, slot, everyone, True)

    for h in range(MOBA_HEADS):
        scores(0, h, 0)
    lax.fori_loop(0, own // MOBA_UNROLL, past_group, 0)
    done = (own // MOBA_UNROLL) * MOBA_UNROLL
    for u in range(MOBA_UNROLL - 1):

        @pl.when(own - done > u)
        def _():
            past(done + u, u % 2)

    @pl.when(own % 2 == 0)
    def _():
        own_block(0)

    @pl.when(own % 2 == 1)
    def _():
        own_block(1)

    for h in range(MOBA_HEADS):
        out_ref[h * MOBA_DH:(h + 1) * MOBA_DH, :] = acc_ref[h, 0:MOBA_DH, :] / acc_ref[h, MOBA_DH:MOBA_DH + 1, :]
    o_ref[...] = out_ref[...].T


def _moba(qt3, k3, vt3, kmean):
    nblk = k3.shape[0]
    seq = nblk * MOBA_BLOCK
    return pl.pallas_call(
        _moba_kernel,
        grid=(nblk,),
        in_specs=[pl.BlockSpec((1, MOBA_W, MOBA_BLOCK), lambda i: (i, 0, 0)),
                  _const_spec((nblk, MOBA_BLOCK, MOBA_W)), _const_spec((nblk, MOBA_W, MOBA_BLOCK)),
                  _const_spec((nblk, MOBA_W))],
        out_specs=pl.BlockSpec((MOBA_BLOCK, MOBA_W), lambda i: (i, 0)),
        out_shape=jax.ShapeDtypeStruct((seq, MOBA_W), F32),
        scratch_shapes=[pltpu.VMEM((MOBA_HEADS, nblk, MOBA_BLOCK), F32),
                        pltpu.VMEM((MOBA_HEADS, LANES, MOBA_BLOCK), BF16),
                        pltpu.VMEM((MOBA_HEADS, 1, MOBA_BLOCK), F32),
                        pltpu.VMEM((MOBA_HEADS, MOBA_PV_ROWS, MOBA_BLOCK), F32),
                        pltpu.VMEM((MOBA_W, MOBA_BLOCK), F32),
                        pltpu.VMEM((2 * MOBA_HEADS, MOBA_BLOCK, MOBA_BLOCK), BF16)],
        compiler_params=_params("parallel"),
        name="moba",
    )(qt3, k3, vt3, kmean)


def _merge_kernel(x_ref, g_ref, gla_ref, moba_ref, loc_ref, halo_ref, wpool_ref, pscale_ref, gsgu_ref, wsgu_ref,
                  bsgu_ref, wgate_ref, wbr_ref, wout_ref, y_ref, zpad_ref):
    x = x_ref[...]
    h = _rms(x, g_ref[...]).astype(BF16)

    def gated(b, branch):
        gate = _sigmoid(jnp.dot(h, wgate_ref[:, b * D_MODEL:(b + 1) * D_MODEL], preferred_element_type=F32))
        return gate * jnp.dot(branch.astype(BF16), wbr_ref[b], preferred_element_type=F32)

    merged = gated(0, gla_ref[...]) + gated(3, moba_ref[...])
    pool, sgu = _local_mixers(loc_ref, halo_ref, wpool_ref, pscale_ref, gsgu_ref, wsgu_ref, bsgu_ref, zpad_ref)
    merged = merged + gated(1, pool) + gated(2, sgu)
    y_ref[...] = x + jnp.dot(merged.astype(BF16), wout_ref[...], preferred_element_type=F32)


def _merge(x, g, o_gla, o_moba, loc, wpool_bd, pscale, gsgu, wsgu, bsgu_full, wgate, wbr, wout):
    seq = x.shape[0]
    per = ROW_TILE // HALO
    row = lambda w: pl.BlockSpec((ROW_TILE, w), lambda i: (i, 0))
    return pl.pallas_call(
        _merge_kernel,
        grid=(seq // ROW_TILE,),
        in_specs=[row(D_MODEL), _const_spec((1, D_MODEL)), row(BRANCH_W), row(BRANCH_W), row(LOC_W),
                  pl.BlockSpec((HALO, BRANCH_W), lambda i: (jnp.maximum(i * per - 1, 0), 0)),
                  _const_spec((BRANCH_W, BRANCH_W)), _const_spec((1, BRANCH_W)), _const_spec((1, BRANCH_W)),
                  _const_spec((SGU_GROUPS, SGU_CHUNK, SGU_CHUNK)), _const_spec((SGU_CHUNK, BRANCH_W)),
                  _const_spec((D_MODEL, N_BRANCH * D_MODEL)), _const_spec((N_BRANCH, BRANCH_W, D_MODEL)),
                  _const_spec((D_MODEL, D_MODEL))],
        out_specs=row(D_MODEL),
        out_shape=jax.ShapeDtypeStruct((seq, D_MODEL), F32),
        scratch_shapes=[pltpu.VMEM((HALO + ROW_TILE, BRANCH_W), F32)],
        compiler_params=_params("parallel"),
        name="merge",
    )(x, g, o_gla, o_moba, loc, loc, wpool_bd, pscale, gsgu, wsgu, bsgu_full, wgate, wbr, wout)


def _ffn_kernel(x_ref, halo_ref, g_ref, wup_ref, cw_ref, cb_ref, wdown_ref, gfin_ref, y_ref,
                h_ref, acc_ref, up_ref, act_ref, *, final_norm):
    rows = x_ref.shape[0]
    x = x_ref[...]
    h_halo = _rms(halo_ref[...], g_ref[...])
    h_ref[0:HALO, :] = jnp.where(pl.program_id(0) > 0, h_halo, 0.0).astype(BF16)
    h_ref[HALO:HALO + rows, :] = _rms(x, g_ref[...]).astype(BF16)
    n_chunks = D_FF // FF_CHUNK

    def up_proj(c, slot):
        for half in range(2):
            cols = pl.ds(half * D_FF + c * FF_CHUNK, FF_CHUNK)
            up_ref[slot, half] = jnp.dot(h_ref[...], wup_ref[:, cols], preferred_element_type=F32)

    def conv(c, slot, half):
        cols = pl.ds(half * D_FF + c * FF_CHUNK, FF_CHUNK)
        cw = cw_ref[:, cols]
        return (up_ref[slot, half, HALO - 2:HALO - 2 + rows, :] * cw[0:1] +
                up_ref[slot, half, HALO - 1:HALO - 1 + rows, :] * cw[1:2] +
                up_ref[slot, half, HALO:HALO + rows, :] * cw[2:3] + cb_ref[:, cols])

    acc_ref[...] = x
    up_proj(0, 0)
    for c in range(n_chunks + 1):
        if c + 1 < n_chunks:
            up_proj(c + 1, (c + 1) % 2)
        if c >= 1:
            acc_ref[...] += jnp.dot(act_ref[(c - 1) % 2], wdown_ref[(c - 1) * FF_CHUNK:c * FF_CHUNK, :],
                                    preferred_element_type=F32)
        if c < n_chunks:
            a = conv(c, c % 2, 0)
            b = conv(c, c % 2, 1)
            act_ref[c % 2] = (a * _sigmoid(a) * b).astype(BF16)
    y = acc_ref[...]
    y_ref[...] = _rms(y, gfin_ref[...]) if final_norm else y


def _ffn(x, g, wup, cw, cb, wdown, gfin, final_norm):
    seq = x.shape[0]
    per = FFN_TILE // HALO
    row = pl.BlockSpec((FFN_TILE, D_MODEL), lambda i: (i, 0))
    return pl.pallas_call(
        functools.partial(_ffn_kernel, final_norm=final_norm),
        grid=(seq // FFN_TILE,),
        in_specs=[row, pl.BlockSpec((HALO, D_MODEL), lambda i: (jnp.maximum(i * per - 1, 0), 0)),
                  _const_spec((1, D_MODEL)), _const_spec((D_MODEL, 2 * D_FF)), _const_spec((3, 2 * D_FF)),
                  _const_spec((1, 2 * D_FF)), _const_spec((D_FF, D_MODEL)), _const_spec((1, D_MODEL))],
        out_specs=row,
        out_shape=jax.ShapeDtypeStruct((seq, D_MODEL), F32),
        scratch_shapes=[pltpu.VMEM((HALO + FFN_TILE, D_MODEL), BF16), pltpu.VMEM((FFN_TILE, D_MODEL), F32),
                        pltpu.VMEM((2, 2, HALO + FFN_TILE, FF_CHUNK), F32),
                        pltpu.VMEM((2, FFN_TILE, FF_CHUNK), BF16)],
        compiler_params=_params("parallel"),
        name="conv_ffn",
    )(x, x, g, wup, cw, cb, wdown, gfin)


def _pack_w_in_kernel(w_ref, o_ref):
    n_gate_end = C_GA + GLA_RANK
    o_ref[:, 0:n_gate_end] = w_ref[0, :, 0:n_gate_end].astype(BF16)
    o_ref[:, n_gate_end:C_LOC] = jnp.zeros((o_ref.shape[0], C_LOC - n_gate_end), BF16)
    o_ref[:, C_LOC:C_END] = w_ref[0, :, n_gate_end:].astype(BF16)


def _pack_w_in(w_in, layer):
    _, rows, cols = w_in.shape
    return pl.pallas_call(
        _pack_w_in_kernel,
        grid=(rows // CAST_ROWS,),
        in_specs=[pl.BlockSpec((1, CAST_ROWS, cols), lambda i: (layer, i, 0))],
        out_specs=pl.BlockSpec((CAST_ROWS, C_END), lambda i: (i, 0)),
        out_shape=jax.ShapeDtypeStruct((rows, C_END), BF16),
        compiler_params=_params("parallel"),
        name="pack_w_in",
    )(w_in)


def _cast_kernel(w_ref, o_ref):
    o_ref[...] = w_ref[0].astype(o_ref.dtype)


def _layer_bf16(w, layer):
    _, rows, cols = w.shape
    block = min(rows, CAST_ROWS)
    return pl.pallas_call(
        _cast_kernel,
        grid=(rows // block,),
        in_specs=[pl.BlockSpec((1, block, cols), lambda i: (layer, i, 0))],
        out_specs=pl.BlockSpec((block, cols), lambda i: (i, 0)),
        out_shape=jax.ShapeDtypeStruct((rows, cols), BF16),
        compiler_params=_params("parallel"),
        name="cast_bf16",
    )(w)


def _block_diag(w):
    g, a, b = w.shape
    eye = jnp.eye(g, dtype=w.dtype)
    return (eye[:, None, :, None] * w[:, :, None, :]).reshape(g * a, g * b)


def kernel(x, positions, g_mix, w_in, w_gla_a, b_gla_a, g_gla_norm, w_pool, pool_scale, g_sgu, w_sgu, b_sgu,
           w_gate, w_branch, w_out, g_ffn, w_up, conv_w, conv_b, w_down, g_final):
    bsz, seq, d = x.shape
    assert bsz == 1 and d == D_MODEL and seq % FFN_TILE == 0 and seq % ROW_TILE == 0 and seq % GLA_TILE == 0
    depth = w_in.shape[0]
    xs = x[0]
    pos = positions.reshape(seq, 1)
    half = MOBA_DH // 2
    inv_freq = ROPE_THETA ** (-jnp.arange(half, dtype=F32) / half)
    invf = jnp.tile(inv_freq, LANES // half)[None, :]
    row = lambda a: a.reshape(1, -1)

    for l in range(depth):
        wa2 = jnp.concatenate([w_gla_a[l], jnp.zeros((LANES - GLA_RANK, GLA_QK), F32)], axis=0)
        gla_in, loc, qt3, k, vt3, kmean = _inproj(xs, pos, row(g_mix[l]), _pack_w_in(w_in, l), wa2,
                                                  row(b_gla_a[l]), invf)
        o_gla = _gla(gla_in, row(jnp.tile(g_gla_norm[l], GLA_HEADS)))
        o_moba = _moba(qt3, k.reshape(seq // MOBA_BLOCK, MOBA_BLOCK, MOBA_W), vt3,
                       kmean.reshape(seq // MOBA_BLOCK, MOBA_W))
        xs = _merge(xs, row(g_mix[l]), o_gla, o_moba, loc, _block_diag(w_pool[l]).astype(BF16),
                    row(pool_scale[l]), row(g_sgu[l]), w_sgu[l], jnp.repeat(b_sgu[l].T, SGU_GW, axis=1),
                    _layer_bf16(w_gate, l),
                    _layer_bf16(w_branch.reshape(depth, N_BRANCH * BRANCH_W, D_MODEL), l).reshape(
                        N_BRANCH, BRANCH_W, D_MODEL),
                    _layer_bf16(w_out, l))
        xs = _ffn(xs, row(g_ffn[l]), _layer_bf16(w_up, l), conv_w[l], row(conv_b[l]), _layer_bf16(w_down, l),
                  row(g_final), final_norm=(l == depth - 1))
    return xs[None]
```

```python
import functools

import numpy as np
import jax
import jax.numpy as jnp
from jax import lax
from jax.experimental import pallas as pl
from jax.experimental.pallas import tpu as pltpu

F32 = jnp.float32
BF16 = jnp.bfloat16
HIGHEST = lax.Precision.HIGHEST

D_MODEL = 1024
N_BRANCH = 4
BRANCH_W = 256
GLA_HEADS, GLA_DK, GLA_DV, GLA_RANK, GLA_TAU = 4, 32, 64, 16, 16.0
GLA_QK = GLA_HEADS * GLA_DK
GLA_V = GLA_HEADS * GLA_DV
POOL_WINDOWS = (2, 4, 8, 16)
POOL_GW = BRANCH_W // len(POOL_WINDOWS)
SGU_CHUNK, SGU_GROUPS = 128, 4
SGU_GW = BRANCH_W // SGU_GROUPS
MOBA_HEADS, MOBA_DH, MOBA_BLOCK, MOBA_TOPK = 4, 64, 256, 3
MOBA_W = MOBA_HEADS * MOBA_DH
MOBA_PV_ROWS = MOBA_DH + 16
MOBA_UNROLL = 4
MOBA_MAX_PARTS = 8
ROPE_THETA = 10000.0
D_FF = 2816
EPS = 1e-6

LANES = 128
VMEM_LIMIT_BYTES = 56 * 1024 * 1024

ROW_TILE = 512
FFN_TILE = 512
GLA_TILE = 256
GLA_SUB = 16
HALO = 16
FF_CHUNK = 256
CAST_ROWS = 256
NEG_BIG = -1e30
LOG2_E = float(np.log2(np.e))

C_GLA = 0
C_GA = C_GLA + 2 * GLA_QK + 2 * GLA_V
C_LOC = C_GA + LANES
C_MQ = C_LOC + BRANCH_W + 2 * BRANCH_W
C_MK = C_MQ + MOBA_W
C_MV = C_MK + MOBA_W
C_END = C_MV + MOBA_W
GLA_IN_W = C_GA + GLA_QK
LOC_W = 3 * BRANCH_W


def _rms(x, g):
    return x * lax.rsqrt(jnp.mean(x * x, axis=-1, keepdims=True) + EPS) * g


def _sigmoid(x):
    return 1.0 / (1.0 + jnp.exp(-x))


def _params(*sem):
    return pltpu.CompilerParams(dimension_semantics=sem, vmem_limit_bytes=VMEM_LIMIT_BYTES)


def _const_spec(shape):
    return pl.BlockSpec(shape, lambda i: (0,) * len(shape), pipeline_mode=pl.Buffered(1))


def _rope_tile(x, cos, sin, first_half):
    partner = jnp.where(first_half, -pltpu.roll(x, LANES - MOBA_DH // 2, axis=1),
                        pltpu.roll(x, MOBA_DH // 2, axis=1))
    return x * cos + partner * sin


def _inproj_kernel(x_ref, pos_ref, g_ref, w_ref, wa2_ref, ba_ref, invf_ref,
                   gla_ref, loc_ref, qt_ref, k_ref, vt_ref, kmean_ref):
    rows = x_ref.shape[0]
    h = _rms(x_ref[...], g_ref[...]).astype(BF16)
    proj = jnp.dot(h, w_ref[...], preferred_element_type=F32)
    proj_m = proj[:, C_MQ:C_END]
    loc_ref[...] = proj[:, C_LOC:C_MQ]

    ang = pos_ref[...].astype(F32) * invf_ref[...]
    cos, sin = jnp.cos(ang), jnp.sin(ang)
    lane = lax.broadcasted_iota(jnp.int32, (rows, LANES), 1)
    first_half = (lane % MOBA_DH) < (MOBA_DH // 2)

    def roped(c0):
        return jnp.concatenate(
            [_rope_tile(proj_m[:, c0 + t * LANES:c0 + (t + 1) * LANES], cos, sin, first_half)
             for t in range(MOBA_W // LANES)], axis=1)

    q = roped(0) * (MOBA_DH ** -0.5)
    k = roped(MOBA_W)
    v = proj_m[:, 2 * MOBA_W:]

    proj_g = proj[:, C_GLA:C_LOC]
    gla_ref[:, 0:C_GA] = proj_g[:, 0:C_GA]
    logit = jnp.dot(proj_g[:, C_GA:C_LOC], wa2_ref[...], precision=HIGHEST,
                    preferred_element_type=F32) + ba_ref[...]
    log_sig = jnp.minimum(logit, 0.0) - jnp.log(1.0 + jnp.exp(-jnp.abs(logit)))
    gla_ref[:, C_GA:GLA_IN_W] = log_sig * (1.0 / GLA_TAU)

    k_ref[...] = k.astype(BF16)
    for r in range(rows // MOBA_BLOCK):
        sl = slice(r * MOBA_BLOCK, (r + 1) * MOBA_BLOCK)
        qt_ref[r] = q[sl].T
        vt_ref[r] = v[sl].T.astype(BF16)
        kmean_ref[0, r:r + 1, :] = jnp.mean(k[sl], axis=0, keepdims=True)


def _inproj(x, pos, g, w_cat, wa2, ba, invf):
    seq = x.shape[0]
    nblk = ROW_TILE // MOBA_BLOCK
    row = lambda w: pl.BlockSpec((ROW_TILE, w), lambda i: (i, 0))
    blk3 = pl.BlockSpec((nblk, MOBA_W, MOBA_BLOCK), lambda i: (i, 0, 0))
    return pl.pallas_call(
        _inproj_kernel,
        grid=(seq // ROW_TILE,),
        in_specs=[row(D_MODEL), row(1), _const_spec((1, D_MODEL)), _const_spec((D_MODEL, C_END)),
                  _const_spec((LANES, GLA_QK)), _const_spec((1, GLA_QK)), _const_spec((1, LANES))],
        out_specs=[row(GLA_IN_W), row(LOC_W), blk3, row(MOBA_W), blk3,
                   pl.BlockSpec((1, nblk, MOBA_W), lambda i: (i, 0, 0))],
        out_shape=[jax.ShapeDtypeStruct((seq, GLA_IN_W), F32),
                   jax.ShapeDtypeStruct((seq, LOC_W), F32),
                   jax.ShapeDtypeStruct((seq // MOBA_BLOCK, MOBA_W, MOBA_BLOCK), F32),
                   jax.ShapeDtypeStruct((seq, MOBA_W), BF16),
                   jax.ShapeDtypeStruct((seq // MOBA_BLOCK, MOBA_W, MOBA_BLOCK), BF16),
                   jax.ShapeDtypeStruct((seq // ROW_TILE, nblk, MOBA_W), F32)],
        compiler_params=_params("parallel"),
        name="inproj",
    )(x, pos, g, w_cat, wa2, ba, invf)


def _gla_kernel(gin_ref, gnorm_ref, out_ref, state_ref, kpad_ref, vpad_ref, bpad_ref,
                qb_ref, ke_ref, eb_ref, o_ref, kv_ref):
    rows = gin_ref.shape[0]

    @pl.when(pl.program_id(0) == 0)
    def _():
        state_ref[...] = jnp.zeros_like(state_ref)
        kpad_ref[0:HALO, :] = jnp.zeros((HALO, GLA_QK), F32)
        vpad_ref[0:HALO, :] = jnp.zeros((HALO, GLA_V), F32)
        bpad_ref[0:HALO, :] = jnp.zeros((HALO, GLA_QK), F32)

    q = gin_ref[:, 0:GLA_QK] * (GLA_DK ** -0.5)
    k = gin_ref[:, GLA_QK:2 * GLA_QK]
    v = gin_ref[:, 2 * GLA_QK:2 * GLA_QK + GLA_V]
    la = gin_ref[:, C_GA:GLA_IN_W]

    rowmod = lax.broadcasted_iota(jnp.int32, (rows, GLA_QK), 0) % GLA_SUB
    b = la
    shift = 1
    while shift < GLA_SUB:
        b = b + jnp.where(rowmod >= shift, pltpu.roll(b, shift, axis=0), 0.0)
        shift *= 2
    b_last = jnp.broadcast_to(b.reshape(rows // GLA_SUB, GLA_SUB, GLA_QK)[:, GLA_SUB - 1:GLA_SUB, :],
                              (rows // GLA_SUB, GLA_SUB, GLA_QK)).reshape(rows, GLA_QK)
    eb = jnp.exp(b)
    qb_ref[...] = (q * eb).astype(BF16)
    ke_ref[...] = (k * jnp.exp(b_last - b)).astype(BF16)
    eb_ref[...] = eb

    kpad_ref[HALO:HALO + rows, :] = k
    vpad_ref[HALO:HALO + rows, :] = v
    bpad_ref[HALO:HALO + rows, :] = b

    head_sum = (lax.broadcasted_iota(jnp.int32, (GLA_QK, GLA_V), 0) // GLA_DK ==
                lax.broadcasted_iota(jnp.int32, (GLA_QK, GLA_V), 1) // GLA_DV).astype(BF16)
    o = jnp.zeros((rows, GLA_V), F32)
    for d in range(GLA_SUB):
        lo = HALO - d
        kd = kpad_ref[lo:lo + rows, :]
        bd = bpad_ref[lo:lo + rows, :]
        vd = vpad_ref[lo:lo + rows, :]
        decay = jnp.exp(jnp.where(rowmod >= d, b - bd, NEG_BIG))
        p = (q * kd * decay).astype(BF16)
        o = o + jnp.dot(p, head_sum, preferred_element_type=F32) * vd
    o_ref[...] = o

    state_mask = (lax.broadcasted_iota(jnp.int32, (GLA_V, GLA_QK), 0) // GLA_DV ==
                  lax.broadcasted_iota(jnp.int32, (GLA_V, GLA_QK), 1) // GLA_DK)

    steps = rows // GLA_SUB
    for s in range(steps):
        blk = slice(s * GLA_SUB, (s + 1) * GLA_SUB)
        v_b = gin_ref[blk, 2 * GLA_QK:2 * GLA_QK + GLA_V].astype(BF16)
        kv = lax.dot_general(v_b, ke_ref[blk, :], (((0,), (0,)), ((), ())),
                             preferred_element_type=F32)
        kv_ref[s] = jnp.where(state_mask, kv, 0.0)

    state = state_ref[...]
    for s in range(steps):
        blk = slice(s * GLA_SUB, (s + 1) * GLA_SUB)
        o_ref[blk, :] += lax.dot_general(qb_ref[blk, :], state.astype(BF16),
                                         (((1,), (1,)), ((), ())), preferred_element_type=F32)
        state = state * eb_ref[(s + 1) * GLA_SUB - 1:(s + 1) * GLA_SUB, :] + kv_ref[s]
    state_ref[...] = state

    o = o_ref[...]
    head_mean = (lax.broadcasted_iota(jnp.int32, (GLA_V, GLA_V), 0) // GLA_DV ==
                 lax.broadcasted_iota(jnp.int32, (GLA_V, GLA_V), 1) // GLA_DV).astype(F32) * (1.0 / GLA_DV)
    ms = jnp.dot(o * o, head_mean, precision=HIGHEST, preferred_element_type=F32)
    r = gin_ref[:, 2 * GLA_QK + GLA_V:C_GA]
    out_ref[...] = o * lax.rsqrt(ms + EPS) * gnorm_ref[...] * (r * _sigmoid(r))


def _gla(gla_in, gnorm):
    seq = gla_in.shape[0]
    return pl.pallas_call(
        _gla_kernel,
        grid=(seq // GLA_TILE,),
        in_specs=[pl.BlockSpec((GLA_TILE, GLA_IN_W), lambda i: (i, 0)), _const_spec((1, GLA_V))],
        out_specs=pl.BlockSpec((GLA_TILE, GLA_V), lambda i: (i, 0)),
        out_shape=jax.ShapeDtypeStruct((seq, GLA_V), F32),
        scratch_shapes=[pltpu.VMEM((GLA_V, GLA_QK), F32),
                        pltpu.VMEM((HALO + GLA_TILE, GLA_QK), F32),
                        pltpu.VMEM((HALO + GLA_TILE, GLA_V), F32),
                        pltpu.VMEM((HALO + GLA_TILE, GLA_QK), F32),
                        pltpu.VMEM((GLA_TILE, GLA_QK), BF16),
                        pltpu.VMEM((GLA_TILE, GLA_QK), BF16),
                        pltpu.VMEM((GLA_TILE, GLA_QK), F32),
                        pltpu.VMEM((GLA_TILE, GLA_V), F32),
                        pltpu.VMEM((GLA_TILE // GLA_SUB, GLA_V, GLA_QK), F32)],
        compiler_params=_params("arbitrary"),
        name="gla",
    )(gla_in, gnorm)


def _local_mixers(loc_ref, halo_ref, wpool_ref, pscale_ref, gsgu_ref, wsgu_ref, bsgu_ref, zpad_ref):
    rows = loc_ref.shape[0]
    i = pl.program_id(0)

    z = loc_ref[:, 0:BRANCH_W]
    zpad_ref[0:HALO, :] = jnp.where(i > 0, halo_ref[...], 0.0)
    zpad_ref[HALO:HALO + rows, :] = z
    t = i * rows + lax.broadcasted_iota(jnp.int32, (rows, BRANCH_W), 0)
    group = lax.broadcasted_iota(jnp.int32, (rows, BRANCH_W), 1) // POOL_GW
    acc = zpad_ref[...]
    mean = jnp.zeros_like(z)
    width = 1
    for g, w in enumerate(POOL_WINDOWS):
        while width < w:
            acc = acc + pltpu.roll(acc, width, axis=0)
            width *= 2
        count = jnp.minimum(t + 1, w).astype(F32)
        mean = jnp.where(group == g, acc[HALO:] / count, mean)
    mixed = (mean - z).astype(BF16)
    pool = jnp.dot(mixed, wpool_ref[...], preferred_element_type=F32) * pscale_ref[...]

    zs = loc_ref[:, BRANCH_W:LOC_W]
    gz = 0.5 * zs * (1.0 + jnp.tanh(np.sqrt(2.0 / np.pi).astype(np.float32) * (zs + 0.044715 * (zs * zs * zs))))
    u = gz[:, 0:BRANCH_W]
    vn = _rms(gz[:, BRANCH_W:], gsgu_ref[...]).astype(BF16)
    tril = (lax.broadcasted_iota(jnp.int32, (SGU_CHUNK, SGU_CHUNK), 0) >=
            lax.broadcasted_iota(jnp.int32, (SGU_CHUNK, SGU_CHUNK), 1))
    w_all = jnp.concatenate([jnp.where(tril, wsgu_ref[g], 0.0).astype(BF16) for g in range(SGU_GROUPS)], axis=1)
    cgroup = lax.broadcasted_iota(jnp.int32, (SGU_CHUNK, BRANCH_W), 1) // SGU_GW
    sgu = []
    for c in range(rows // SGU_CHUNK):
        sl = slice(c * SGU_CHUNK, (c + 1) * SGU_CHUNK)
        v_grp = jnp.concatenate([jnp.where(cgroup == g, vn[sl], 0.0) for g in range(SGU_GROUPS)], axis=0)
        sv = jnp.dot(w_all, v_grp, preferred_element_type=F32) + bsgu_ref[...]
        sgu.append(u[sl] * sv)
    return pool, jnp.concatenate(sgu, axis=0)


def _moba_kernel(qt_ref, k_ref, vt_ref, kmean_ref, o_ref, sel_ref, qtb_ref, m_ref, acc_ref, out_ref, s_ref):
    own = pl.program_id(0)
    nblk = k_ref.shape[0]
    blk_id = lax.broadcasted_iota(jnp.int32, (nblk, MOBA_BLOCK), 0)
    pair_row = lax.broadcasted_iota(jnp.int32, (LANES, MOBA_BLOCK), 0) // MOBA_DH
    key_pos = lax.broadcasted_iota(jnp.int32, (MOBA_BLOCK, MOBA_BLOCK), 0)
    q_pos = lax.broadcasted_iota(jnp.int32, (MOBA_BLOCK, MOBA_BLOCK), 1)
    ones_rows = (lax.broadcasted_iota(jnp.int32, (MOBA_PV_ROWS - MOBA_DH, MOBA_BLOCK), 0) == 0).astype(BF16)

    for h in range(MOBA_HEADS):
        lanes = pl.ds((h // 2) * LANES, LANES)
        qt = jnp.where(pair_row == (h % 2), qt_ref[0, (h // 2) * LANES:(h // 2 + 1) * LANES, :], 0.0)
        qtb_ref[h] = (qt * LOG2_E).astype(BF16)

        gate = jnp.dot(kmean_ref[:, lanes], qt, precision=HIGHEST, preferred_element_type=F32)
        gate = jnp.where(blk_id < own, gate, -jnp.inf)
        sel = jnp.zeros((nblk, MOBA_BLOCK), F32)
        for _ in range(MOBA_TOPK):
            best = jnp.max(gate, axis=0, keepdims=True)
            first = jnp.min(jnp.where(gate == best, blk_id, nblk), axis=0, keepdims=True)
            pick = (blk_id == first) & (best > -jnp.inf)
            sel = jnp.where(pick, 1.0, sel)
            gate = jnp.where(pick, -jnp.inf, gate)
        sel_ref[h] = sel
        m_ref[h] = jnp.full((1, MOBA_BLOCK), NEG_BIG, F32)
        acc_ref[h] = jnp.zeros((MOBA_PV_ROWS, MOBA_BLOCK), F32)

    def scores(j, h, slot):
        lanes = pl.ds((h // 2) * LANES, LANES)
        s_ref[slot * MOBA_HEADS + h] = jnp.dot(k_ref[j, :, lanes], qtb_ref[h],
                                               preferred_element_type=F32).astype(BF16)

    def attend(j, h, slot, selected, causal):
        s = s_ref[slot * MOBA_HEADS + h]
        if causal:
            s = jnp.where(key_pos <= q_pos, s, NEG_BIG)
        m_run = m_ref[h]
        parts = s.reshape(MOBA_MAX_PARTS, MOBA_BLOCK // MOBA_MAX_PARTS, MOBA_BLOCK)
        blk_max = jnp.max(jnp.max(parts, axis=0), axis=0, keepdims=True)
        m_new = jnp.maximum(m_run, jnp.where(selected, blk_max.astype(F32), NEG_BIG))
        shift = jnp.where(selected, m_new, -NEG_BIG).astype(BF16)
        p = jnp.exp2(s - shift)
        v_aug = jnp.concatenate([vt_ref[j, pl.ds(h * MOBA_DH, MOBA_DH), :], ones_rows], axis=0)
        pv = jnp.dot(v_aug, p, preferred_element_type=F32)
        acc_ref[h] = acc_ref[h] * jnp.exp2(m_run - m_new) + pv
        m_ref[h] = m_new

    def past(j, slot):
        for pair in range(MOBA_HEADS // 2):
            scores(j + 1, 2 * pair, 1 - slot)
            scores(j + 1, 2 * pair + 1, 1 - slot)
            for h in (2 * pair, 2 * pair + 1):
                attend(j, h, slot, sel_ref[h, pl.ds(j, 1), :] > 0.0, False)

    def past_group(t, carry):
        for u in range(MOBA_UNROLL):
            past(MOBA_UNROLL * t + u, u % 2)
        return carry

    def own_block(slot):
        everyone = jnp.full((1, MOBA_BLOCK), True)
        for h in range(MOBA_HEADS):
            attend(own, h, slot, everyone, True)

    for h in range(MOBA_HEADS):
        scores(0, h, 0)
    lax.fori_loop(0, own // MOBA_UNROLL, past_group, 0)
    done = (own // MOBA_UNROLL) * MOBA_UNROLL
    for u in range(MOBA_UNROLL - 1):

        @pl.when(own - done > u)
        def _():
            past(done + u, u % 2)

    @pl.when(own % 2 == 0)
    def _():
        own_block(0)

    @pl.when(own % 2 == 1)
    def _():
        own_block(1)

    for h in range(MOBA_HEADS):
        out_ref[h * MOBA_DH:(h + 1) * MOBA_DH, :] = acc_ref[h, 0:MOBA_DH, :] / acc_ref[h, MOBA_DH:MOBA_DH + 1, :]
    o_ref[...] = out_ref[...].T


def _moba(qt3, k3, vt3, kmean):
    nblk = k3.shape[0]
    seq = nblk * MOBA_BLOCK
    return pl.pallas_call(
        _moba_kernel,
        grid=(nblk,),
        in_specs=[pl.BlockSpec((1, MOBA_W, MOBA_BLOCK), lambda i: (i, 0, 0)),
                  _const_spec((nblk, MOBA_BLOCK, MOBA_W)), _const_spec((nblk, MOBA_W, MOBA_BLOCK)),
                  _const_spec((nblk, MOBA_W))],
        out_specs=pl.BlockSpec((MOBA_BLOCK, MOBA_W), lambda i: (i, 0)),
        out_shape=jax.ShapeDtypeStruct((seq, MOBA_W), F32),
        scratch_shapes=[pltpu.VMEM((MOBA_HEADS, nblk, MOBA_BLOCK), F32),
                        pltpu.VMEM((MOBA_HEADS, LANES, MOBA_BLOCK), BF16),
                        pltpu.VMEM((MOBA_HEADS, 1, MOBA_BLOCK), F32),
                        pltpu.VMEM((MOBA_HEADS, MOBA_PV_ROWS, MOBA_BLOCK), F32),
                        pltpu.VMEM((MOBA_W, MOBA_BLOCK), F32),
                        pltpu.VMEM((2 * MOBA_HEADS, MOBA_BLOCK, MOBA_BLOCK), BF16)],
        compiler_params=_params("parallel"),
        name="moba",
    )(qt3, k3, vt3, kmean)


def _merge_kernel(x_ref, g_ref, gla_ref, moba_ref, loc_ref, halo_ref, wpool_ref, pscale_ref, gsgu_ref, wsgu_ref,
                  bsgu_ref, wgate_ref, wbr_ref, wout_ref, y_ref, zpad_ref):
    x = x_ref[...]
    h = _rms(x, g_ref[...]).astype(BF16)

    def gated(b, branch):
        gate = _sigmoid(jnp.dot(h, wgate_ref[:, b * D_MODEL:(b + 1) * D_MODEL], preferred_element_type=F32))
        return gate * jnp.dot(branch.astype(BF16), wbr_ref[b], preferred_element_type=F32)

    merged = gated(0, gla_ref[...]) + gated(3, moba_ref[...])
    pool, sgu = _local_mixers(loc_ref, halo_ref, wpool_ref, pscale_ref, gsgu_ref, wsgu_ref, bsgu_ref, zpad_ref)
    merged = merged + gated(1, pool) + gated(2, sgu)
    y_ref[...] = x + jnp.dot(merged.astype(BF16), wout_ref[...], preferred_element_type=F32)


def _merge(x, g, o_gla, o_moba, loc, wpool_bd, pscale, gsgu, wsgu, bsgu_full, wgate, wbr, wout):
    seq = x.shape[0]
    per = ROW_TILE // HALO
    row = lambda w: pl.BlockSpec((ROW_TILE, w), lambda i: (i, 0))
    return pl.pallas_call(
        _merge_kernel,
        grid=(seq // ROW_TILE,),
        in_specs=[row(D_MODEL), _const_spec((1, D_MODEL)), row(BRANCH_W), row(BRANCH_W), row(LOC_W),
                  pl.BlockSpec((HALO, BRANCH_W), lambda i: (jnp.maximum(i * per - 1, 0), 0)),
                  _const_spec((BRANCH_W, BRANCH_W)), _const_spec((1, BRANCH_W)), _const_spec((1, BRANCH_W)),
                  _const_spec((SGU_GROUPS, SGU_CHUNK, SGU_CHUNK)), _const_spec((SGU_CHUNK, BRANCH_W)),
                  _const_spec((D_MODEL, N_BRANCH * D_MODEL)), _const_spec((N_BRANCH, BRANCH_W, D_MODEL)),
                  _const_spec((D_MODEL, D_MODEL))],
        out_specs=row(D_MODEL),
        out_shape=jax.ShapeDtypeStruct((seq, D_MODEL), F32),
        scratch_shapes=[pltpu.VMEM((HALO + ROW_TILE, BRANCH_W), F32)],
        compiler_params=_params("parallel"),
        name="merge",
    )(x, g, o_gla, o_moba, loc, loc, wpool_bd, pscale, gsgu, wsgu, bsgu_full, wgate, wbr, wout)


def _ffn_kernel(x_ref, halo_ref, g_ref, wup_ref, cw_ref, cb_ref, wdown_ref, gfin_ref, y_ref,
                h_ref, acc_ref, up_ref, act_ref, *, final_norm):
    rows = x_ref.shape[0]
    x = x_ref[...]
    h_halo = _rms(halo_ref[...], g_ref[...])
    h_ref[0:HALO, :] = jnp.where(pl.program_id(0) > 0, h_halo, 0.0).astype(BF16)
    h_ref[HALO:HALO + rows, :] = _rms(x, g_ref[...]).astype(BF16)
    n_chunks = D_FF // FF_CHUNK

    def up_proj(c, slot):
        for half in range(2):
            cols = pl.ds(half * D_FF + c * FF_CHUNK, FF_CHUNK)
            up_ref[slot, half] = jnp.dot(h_ref[...], wup_ref[:, cols], preferred_element_type=F32)

    def conv(c, slot, half):
        cols = pl.ds(half * D_FF + c * FF_CHUNK, FF_CHUNK)
        cw = cw_ref[:, cols]
        return (up_ref[slot, half, HALO - 2:HALO - 2 + rows, :] * cw[0:1] +
                up_ref[slot, half, HALO - 1:HALO - 1 + rows, :] * cw[1:2] +
                up_ref[slot, half, HALO:HALO + rows, :] * cw[2:3] + cb_ref[:, cols])

    acc_ref[...] = x
    up_proj(0, 0)
    for c in range(n_chunks + 1):
        if c + 1 < n_chunks:
            up_proj(c + 1, (c + 1) % 2)
        if c >= 1:
            acc_ref[...] += jnp.dot(act_ref[(c - 1) % 2], wdown_ref[(c - 1) * FF_CHUNK:c * FF_CHUNK, :],
                                    preferred_element_type=F32)
        if c < n_chunks:
            a = conv(c, c % 2, 0)
            b = conv(c, c % 2, 1)
            act_ref[c % 2] = (a * _sigmoid(a) * b).astype(BF16)
    y = acc_ref[...]
    y_ref[...] = _rms(y, gfin_ref[...]) if final_norm else y


def _ffn(x, g, wup, cw, cb, wdown, gfin, final_norm):
    seq = x.shape[0]
    per = FFN_TILE // HALO
    row = pl.BlockSpec((FFN_TILE, D_MODEL), lambda i: (i, 0))
    return pl.pallas_call(
        functools.partial(_ffn_kernel, final_norm=final_norm),
        grid=(seq // FFN_TILE,),
        in_specs=[row, pl.BlockSpec((HALO, D_MODEL), lambda i: (jnp.maximum(i * per - 1, 0), 0)),
                  _const_spec((1, D_MODEL)), _const_spec((D_MODEL, 2 * D_FF)), _const_spec((3, 2 * D_FF)),
                  _const_spec((1, 2 * D_FF)), _const_spec((D_FF, D_MODEL)), _const_spec((1, D_MODEL))],
        out_specs=row,
        out_shape=jax.ShapeDtypeStruct((seq, D_MODEL), F32),
        scratch_shapes=[pltpu.VMEM((HALO + FFN_TILE, D_MODEL), BF16), pltpu.VMEM((FFN_TILE, D_MODEL), F32),
                        pltpu.VMEM((2, 2, HALO + FFN_TILE, FF_CHUNK), F32),
                        pltpu.VMEM((2, FFN_TILE, FF_CHUNK), BF16)],
        compiler_params=_params("parallel"),
        name="conv_ffn",
    )(x, x, g, wup, cw, cb, wdown, gfin)


def _pack_w_in_kernel(wt_ref, o_ref):
    n_gate_end = C_GA + GLA_RANK
    d_model = wt_ref.shape[2]
    for c in range(C_END // LANES):
        if (c + 1) * LANES <= C_GA:
            cols = wt_ref[0, c * LANES:(c + 1) * LANES, :]
        elif c * LANES == C_GA:
            cols = jnp.concatenate([wt_ref[0, C_GA:n_gate_end, :],
                                    jnp.zeros((LANES - GLA_RANK, d_model), F32)], axis=0)
        else:
            src = n_gate_end + c * LANES - C_LOC
            cols = wt_ref[0, src:src + LANES, :]
        o_ref[:, c * LANES:(c + 1) * LANES] = cols.T.astype(BF16)


def _pack_w_in(w_in, layer):
    _, rows, cols = w_in.shape
    return pl.pallas_call(
        _pack_w_in_kernel,
        grid=(1,),
        in_specs=[pl.BlockSpec((1, cols, rows), lambda i: (layer, 0, 0))],
        out_specs=pl.BlockSpec((rows, C_END), lambda i: (0, 0)),
        out_shape=jax.ShapeDtypeStruct((rows, C_END), BF16),
        compiler_params=_params("arbitrary"),
        name="pack_w_in",
    )(jnp.swapaxes(w_in, 1, 2))


def _cast_kernel(w_ref, o_ref):
    o_ref[...] = w_ref[0].astype(o_ref.dtype)


def _layer_bf16(w, layer):
    _, rows, cols = w.shape
    block = min(rows, CAST_ROWS)
    return pl.pallas_call(
        _cast_kernel,
        grid=(rows // block,),
        in_specs=[pl.BlockSpec((1, block, cols), lambda i: (layer, i, 0))],
        out_specs=pl.BlockSpec((block, cols), lambda i: (i, 0)),
        out_shape=jax.ShapeDtypeStruct((rows, cols), BF16),
        compiler_params=_params("parallel"),
        name="cast_bf16",
    )(w)


def _block_diag(w):
    g, a, b = w.shape
    eye = jnp.eye(g, dtype=w.dtype)
    return (eye[:, None, :, None] * w[:, :, None, :]).reshape(g * a, g * b)


def kernel(x, positions, g_mix, w_in, w_gla_a, b_gla_a, g_gla_norm, w_pool, pool_scale, g_sgu, w_sgu, b_sgu,
           w_gate, w_branch, w_out, g_ffn, w_up, conv_w, conv_b, w_down, g_final):
    bsz, seq, d = x.shape
    assert bsz == 1 and d == D_MODEL and seq % FFN_TILE == 0 and seq % ROW_TILE == 0 and seq % GLA_TILE == 0
    depth = w_in.shape[0]
    xs = x[0]
    pos = positions.reshape(seq, 1)
    half = MOBA_DH // 2
    inv_freq = ROPE_THETA ** (-jnp.arange(half, dtype=F32) / half)
    invf = jnp.tile(inv_freq, LANES // half)[None, :]
    row = lambda a: a.reshape(1, -1)

    for l in range(depth):
        wa2 = jnp.concatenate([w_gla_a[l], jnp.zeros((LANES - GLA_RANK, GLA_QK), F32)], axis=0)
        gla_in, loc, qt3, k, vt3, kmean = _inproj(xs, pos, row(g_mix[l]), _pack_w_in(w_in, l), wa2,
                                                  row(b_gla_a[l]), invf)
        o_gla = _gla(gla_in, row(jnp.tile(g_gla_norm[l], GLA_HEADS)))
        o_moba = _moba(qt3, k.reshape(seq // MOBA_BLOCK, MOBA_BLOCK, MOBA_W), vt3,
                       kmean.reshape(seq // MOBA_BLOCK, MOBA_W))
        xs = _merge(xs, row(g_mix[l]), o_gla, o_moba, loc, _block_diag(w_pool[l]).astype(BF16),
                    row(pool_scale[l]), row(g_sgu[l]), w_sgu[l], jnp.repeat(b_sgu[l].T, SGU_GW, axis=1),
                    _layer_bf16(w_gate, l),
                    _layer_bf16(w_branch.reshape(depth, N_BRANCH * BRANCH_W, D_MODEL), l).reshape(
                        N_BRANCH, BRANCH_W, D_MODEL),
                    _layer_bf16(w_out, l))
        xs = _ffn(xs, row(g_ffn[l]), _layer_bf16(w_up, l), conv_w[l], row(conv_b[l]), _layer_bf16(w_down, l),
                  row(g_final), final_norm=(l == depth - 1))
    return xs[None]
```

```python
import functools

import numpy as np
import jax
import jax.numpy as jnp
from jax import lax
from jax.experimental import pallas as pl
from jax.experimental.pallas import tpu as pltpu

F32 = jnp.float32
BF16 = jnp.bfloat16
HIGHEST = lax.Precision.HIGHEST

D_MODEL = 1024
N_BRANCH = 4
BRANCH_W = 256
GLA_HEADS, GLA_DK, GLA_DV, GLA_RANK, GLA_TAU = 4, 32, 64, 16, 16.0
GLA_QK = GLA_HEADS * GLA_DK
GLA_V = GLA_HEADS * GLA_DV
POOL_WINDOWS = (2, 4, 8, 16)
POOL_GW = BRANCH_W // len(POOL_WINDOWS)
SGU_CHUNK, SGU_GROUPS = 128, 4
SGU_GW = BRANCH_W // SGU_GROUPS
MOBA_HEADS, MOBA_DH, MOBA_BLOCK, MOBA_TOPK = 4, 64, 256, 3
MOBA_W = MOBA_HEADS * MOBA_DH
MOBA_PV_ROWS = MOBA_DH + 16
MOBA_UNROLL = 4
MOBA_MAX_PARTS = 8
MOBA_QBLOCKS = 2
MOBA_QT = MOBA_QBLOCKS * MOBA_BLOCK
ROPE_THETA = 10000.0
D_FF = 2816
EPS = 1e-6

LANES = 128
VMEM_LIMIT_BYTES = 56 * 1024 * 1024

ROW_TILE = 512
FFN_TILE = 512
GLA_TILE = 256
GLA_SUB = 16
HALO = 16
FF_CHUNK = 256
CAST_ROWS = 256
NEG_BIG = -1e30
LOG2_E = float(np.log2(np.e))

C_GLA = 0
C_GA = C_GLA + 2 * GLA_QK + 2 * GLA_V
C_LOC = C_GA + LANES
C_MQ = C_LOC + BRANCH_W + 2 * BRANCH_W
C_MK = C_MQ + MOBA_W
C_MV = C_MK + MOBA_W
C_END = C_MV + MOBA_W
GLA_IN_W = C_GA + GLA_QK
LOC_W = 3 * BRANCH_W


def _rms(x, g):
    return x * lax.rsqrt(jnp.mean(x * x, axis=-1, keepdims=True) + EPS) * g


def _sigmoid(x):
    return 1.0 / (1.0 + jnp.exp(-x))


def _params(*sem):
    return pltpu.CompilerParams(dimension_semantics=sem, vmem_limit_bytes=VMEM_LIMIT_BYTES)


def _const_spec(shape):
    return pl.BlockSpec(shape, lambda i: (0,) * len(shape), pipeline_mode=pl.Buffered(1))


def _rope_tile(x, cos, sin, first_half):
    partner = jnp.where(first_half, -pltpu.roll(x, LANES - MOBA_DH // 2, axis=1),
                        pltpu.roll(x, MOBA_DH // 2, axis=1))
    return x * cos + partner * sin


def _inproj_kernel(x_ref, pos_ref, g_ref, w_ref, wa2_ref, ba_ref, invf_ref,
                   gla_ref, loc_ref, qt_ref, k_ref, vt_ref, kmean_ref):
    rows = x_ref.shape[0]
    h = _rms(x_ref[...], g_ref[...]).astype(BF16)
    proj = jnp.dot(h, w_ref[...], preferred_element_type=F32)
    proj_m = proj[:, C_MQ:C_END]
    loc_ref[...] = proj[:, C_LOC:C_MQ]

    ang = pos_ref[...].astype(F32) * invf_ref[...]
    cos, sin = jnp.cos(ang), jnp.sin(ang)
    lane = lax.broadcasted_iota(jnp.int32, (rows, LANES), 1)
    first_half = (lane % MOBA_DH) < (MOBA_DH // 2)

    def roped(c0):
        return jnp.concatenate(
            [_rope_tile(proj_m[:, c0 + t * LANES:c0 + (t + 1) * LANES], cos, sin, first_half)
             for t in range(MOBA_W // LANES)], axis=1)

    q = roped(0) * (MOBA_DH ** -0.5)
    k = roped(MOBA_W)
    v = proj_m[:, 2 * MOBA_W:]

    proj_g = proj[:, C_GLA:C_LOC]
    gla_ref[:, 0:C_GA] = proj_g[:, 0:C_GA]
    logit = jnp.dot(proj_g[:, C_GA:C_LOC], wa2_ref[...], precision=HIGHEST,
                    preferred_element_type=F32) + ba_ref[...]
    log_sig = jnp.minimum(logit, 0.0) - jnp.log(1.0 + jnp.exp(-jnp.abs(logit)))
    gla_ref[:, C_GA:GLA_IN_W] = log_sig * (1.0 / GLA_TAU)

    k_ref[...] = k.astype(BF16)
    for r in range(rows // MOBA_BLOCK):
        sl = slice(r * MOBA_BLOCK, (r + 1) * MOBA_BLOCK)
        qt_ref[r] = q[sl].T
        vt_ref[r] = v[sl].T.astype(BF16)
        kmean_ref[0, r:r + 1, :] = jnp.mean(k[sl], axis=0, keepdims=True)


def _inproj(x, pos, g, w_cat, wa2, ba, invf):
    seq = x.shape[0]
    nblk = ROW_TILE // MOBA_BLOCK
    row = lambda w: pl.BlockSpec((ROW_TILE, w), lambda i: (i, 0))
    blk3 = pl.BlockSpec((nblk, MOBA_W, MOBA_BLOCK), lambda i: (i, 0, 0))
    return pl.pallas_call(
        _inproj_kernel,
        grid=(seq // ROW_TILE,),
        in_specs=[row(D_MODEL), row(1), _const_spec((1, D_MODEL)), _const_spec((D_MODEL, C_END)),
                  _const_spec((LANES, GLA_QK)), _const_spec((1, GLA_QK)), _const_spec((1, LANES))],
        out_specs=[row(GLA_IN_W), row(LOC_W), blk3, row(MOBA_W), blk3,
                   pl.BlockSpec((1, nblk, MOBA_W), lambda i: (i, 0, 0))],
        out_shape=[jax.ShapeDtypeStruct((seq, GLA_IN_W), F32),
                   jax.ShapeDtypeStruct((seq, LOC_W), F32),
                   jax.ShapeDtypeStruct((seq // MOBA_BLOCK, MOBA_W, MOBA_BLOCK), F32),
                   jax.ShapeDtypeStruct((seq, MOBA_W), BF16),
                   jax.ShapeDtypeStruct((seq // MOBA_BLOCK, MOBA_W, MOBA_BLOCK), BF16),
                   jax.ShapeDtypeStruct((seq // ROW_TILE, nblk, MOBA_W), F32)],
        compiler_params=_params("parallel"),
        name="inproj",
    )(x, pos, g, w_cat, wa2, ba, invf)


def _gla_kernel(gin_ref, gnorm_ref, out_ref, state_ref, kpad_ref, vpad_ref, bpad_ref,
                qb_ref, ke_ref, eb_ref, o_ref, kv_ref):
    rows = gin_ref.shape[0]

    @pl.when(pl.program_id(0) == 0)
    def _():
        state_ref[...] = jnp.zeros_like(state_ref)
        kpad_ref[0:HALO, :] = jnp.zeros((HALO, GLA_QK), F32)
        vpad_ref[0:HALO, :] = jnp.zeros((HALO, GLA_V), F32)
        bpad_ref[0:HALO, :] = jnp.zeros((HALO, GLA_QK), F32)

    q = gin_ref[:, 0:GLA_QK] * (GLA_DK ** -0.5)
    k = gin_ref[:, GLA_QK:2 * GLA_QK]
    v = gin_ref[:, 2 * GLA_QK:2 * GLA_QK + GLA_V]
    la = gin_ref[:, C_GA:GLA_IN_W]

    rowmod = lax.broadcasted_iota(jnp.int32, (rows, GLA_QK), 0) % GLA_SUB
    b = la
    shift = 1
    while shift < GLA_SUB:
        b = b + jnp.where(rowmod >= shift, pltpu.roll(b, shift, axis=0), 0.0)
        shift *= 2
    b_last = jnp.broadcast_to(b.reshape(rows // GLA_SUB, GLA_SUB, GLA_QK)[:, GLA_SUB - 1:GLA_SUB, :],
                              (rows // GLA_SUB, GLA_SUB, GLA_QK)).reshape(rows, GLA_QK)
    eb = jnp.exp(b)
    qb_ref[...] = (q * eb).astype(BF16)
    ke_ref[...] = (k * jnp.exp(b_last - b)).astype(BF16)
    eb_ref[...] = eb

    kpad_ref[HALO:HALO + rows, :] = k
    vpad_ref[HALO:HALO + rows, :] = v
    bpad_ref[HALO:HALO + rows, :] = b

    head_sum = (lax.broadcasted_iota(jnp.int32, (GLA_QK, GLA_V), 0) // GLA_DK ==
                lax.broadcasted_iota(jnp.int32, (GLA_QK, GLA_V), 1) // GLA_DV).astype(BF16)
    o = jnp.zeros((rows, GLA_V), F32)
    for d in range(GLA_SUB):
        lo = HALO - d
        kd = kpad_ref[lo:lo + rows, :]
        bd = bpad_ref[lo:lo + rows, :]
        vd = vpad_ref[lo:lo + rows, :]
        decay = jnp.exp(jnp.where(rowmod >= d, b - bd, NEG_BIG))
        p = (q * kd * decay).astype(BF16)
        o = o + jnp.dot(p, head_sum, preferred_element_type=F32) * vd
    o_ref[...] = o

    state_mask = (lax.broadcasted_iota(jnp.int32, (GLA_V, GLA_QK), 0) // GLA_DV ==
                  lax.broadcasted_iota(jnp.int32, (GLA_V, GLA_QK), 1) // GLA_DK)

    steps = rows // GLA_SUB
    for s in range(steps):
        blk = slice(s * GLA_SUB, (s + 1) * GLA_SUB)
        v_b = gin_ref[blk, 2 * GLA_QK:2 * GLA_QK + GLA_V].astype(BF16)
        kv = lax.dot_general(v_b, ke_ref[blk, :], (((0,), (0,)), ((), ())),
                             preferred_element_type=F32)
        kv_ref[s] = jnp.where(state_mask, kv, 0.0)

    state = state_ref[...]
    for s in range(steps):
        blk = slice(s * GLA_SUB, (s + 1) * GLA_SUB)
        o_ref[blk, :] += lax.dot_general(qb_ref[blk, :], state.astype(BF16),
                                         (((1,), (1,)), ((), ())), preferred_element_type=F32)
        state = state * eb_ref[(s + 1) * GLA_SUB - 1:(s + 1) * GLA_SUB, :] + kv_ref[s]
    state_ref[...] = state

    o = o_ref[...]
    head_mean = (lax.broadcasted_iota(jnp.int32, (GLA_V, GLA_V), 0) // GLA_DV ==
                 lax.broadcasted_iota(jnp.int32, (GLA_V, GLA_V), 1) // GLA_DV).astype(F32) * (1.0 / GLA_DV)
    ms = jnp.dot(o * o, head_mean, precision=HIGHEST, preferred_element_type=F32)
    r = gin_ref[:, 2 * GLA_QK + GLA_V:C_GA]
    out_ref[...] = o * lax.rsqrt(ms + EPS) * gnorm_ref[...] * (r * _sigmoid(r))


def _gla(gla_in, gnorm):
    seq = gla_in.shape[0]
    return pl.pallas_call(
        _gla_kernel,
        grid=(seq // GLA_TILE,),
        in_specs=[pl.BlockSpec((GLA_TILE, GLA_IN_W), lambda i: (i, 0)), _const_spec((1, GLA_V))],
        out_specs=pl.BlockSpec((GLA_TILE, GLA_V), lambda i: (i, 0)),
        out_shape=jax.ShapeDtypeStruct((seq, GLA_V), F32),
        scratch_shapes=[pltpu.VMEM((GLA_V, GLA_QK), F32),
                        pltpu.VMEM((HALO + GLA_TILE, GLA_QK), F32),
                        pltpu.VMEM((HALO + GLA_TILE, GLA_V), F32),
                        pltpu.VMEM((HALO + GLA_TILE, GLA_QK), F32),
                        pltpu.VMEM((GLA_TILE, GLA_QK), BF16),
                        pltpu.VMEM((GLA_TILE, GLA_QK), BF16),
                        pltpu.VMEM((GLA_TILE, GLA_QK), F32),
                        pltpu.VMEM((GLA_TILE, GLA_V), F32),
                        pltpu.VMEM((GLA_TILE // GLA_SUB, GLA_V, GLA_QK), F32)],
        compiler_params=_params("arbitrary"),
        name="gla",
    )(gla_in, gnorm)


def _local_mixers(loc_ref, halo_ref, wpool_ref, pscale_ref, gsgu_ref, wsgu_ref, bsgu_ref, zpad_ref):
    rows = loc_ref.shape[0]
    i = pl.program_id(0)

    z = loc_ref[:, 0:BRANCH_W]
    zpad_ref[0:HALO, :] = jnp.where(i > 0, halo_ref[...], 0.0)
    zpad_ref[HALO:HALO + rows, :] = z
    t = i * rows + lax.broadcasted_iota(jnp.int32, (rows, BRANCH_W), 0)
    group = lax.broadcasted_iota(jnp.int32, (rows, BRANCH_W), 1) // POOL_GW
    acc = zpad_ref[...]
    mean = jnp.zeros_like(z)
    width = 1
    for g, w in enumerate(POOL_WINDOWS):
        while width < w:
            acc = acc + pltpu.roll(acc, width, axis=0)
            width *= 2
        count = jnp.minimum(t + 1, w).astype(F32)
        mean = jnp.where(group == g, acc[HALO:] / count, mean)
    mixed = (mean - z).astype(BF16)
    pool = jnp.dot(mixed, wpool_ref[...], preferred_element_type=F32) * pscale_ref[...]

    zs = loc_ref[:, BRANCH_W:LOC_W]
    gz = 0.5 * zs * (1.0 + jnp.tanh(np.sqrt(2.0 / np.pi).astype(np.float32) * (zs + 0.044715 * (zs * zs * zs))))
    u = gz[:, 0:BRANCH_W]
    vn = _rms(gz[:, BRANCH_W:], gsgu_ref[...]).astype(BF16)
    tril = (lax.broadcasted_iota(jnp.int32, (SGU_CHUNK, SGU_CHUNK), 0) >=
            lax.broadcasted_iota(jnp.int32, (SGU_CHUNK, SGU_CHUNK), 1))
    w_all = jnp.concatenate([jnp.where(tril, wsgu_ref[g], 0.0).astype(BF16) for g in range(SGU_GROUPS)], axis=1)
    cgroup = lax.broadcasted_iota(jnp.int32, (SGU_CHUNK, BRANCH_W), 1) // SGU_GW
    sgu = []
    for c in range(rows // SGU_CHUNK):
        sl = slice(c * SGU_CHUNK, (c + 1) * SGU_CHUNK)
        v_grp = jnp.concatenate([jnp.where(cgroup == g, vn[sl], 0.0) for g in range(SGU_GROUPS)], axis=0)
        sv = jnp.dot(w_all, v_grp, preferred_element_type=F32) + bsgu_ref[...]
        sgu.append(u[sl] * sv)
    return pool, jnp.concatenate(sgu, axis=0)


def _moba_kernel(qt_ref, k_ref, vt_ref, kmean_ref, o_ref, sel_ref, qtb_ref, m_ref, acc_ref, out_ref, s_ref):
    first_own = MOBA_QBLOCKS * pl.program_id(0)
    nblk = k_ref.shape[0]
    blk_id = lax.broadcasted_iota(jnp.int32, (nblk, MOBA_QT), 0)
    pair_row = lax.broadcasted_iota(jnp.int32, (LANES, MOBA_QT), 0) // MOBA_DH
    key_pos = lax.broadcasted_iota(jnp.int32, (MOBA_BLOCK, MOBA_QT), 0)
    q_col = lax.broadcasted_iota(jnp.int32, (MOBA_BLOCK, MOBA_QT), 1)
    q_blk = lax.broadcasted_iota(jnp.int32, (1, MOBA_QT), 1) // MOBA_BLOCK
    ones_rows = (lax.broadcasted_iota(jnp.int32, (MOBA_PV_ROWS - MOBA_DH, MOBA_BLOCK), 0) == 0).astype(BF16)
    qt_all = jnp.concatenate([qt_ref[b] for b in range(MOBA_QBLOCKS)], axis=1)

    for h in range(MOBA_HEADS):
        lanes = pl.ds((h // 2) * LANES, LANES)
        qt = jnp.where(pair_row == (h % 2), qt_all[(h // 2) * LANES:(h // 2 + 1) * LANES, :], 0.0)
        qtb_ref[h] = (qt * LOG2_E).astype(BF16)

        gate = jnp.dot(kmean_ref[:, lanes], qt, precision=HIGHEST, preferred_element_type=F32)
        gate = jnp.where(blk_id < first_own + q_blk, gate, -jnp.inf)
        sel = jnp.zeros((nblk, MOBA_QT), F32)
        for _ in range(MOBA_TOPK):
            best = jnp.max(gate, axis=0, keepdims=True)
            first = jnp.min(jnp.where(gate == best, blk_id, nblk), axis=0, keepdims=True)
            pick = (blk_id == first) & (best > -jnp.inf)
            sel = jnp.where(pick, 1.0, sel)
            gate = jnp.where(pick, -jnp.inf, gate)
        sel_ref[h] = sel
        m_ref[h] = jnp.full((1, MOBA_QT), NEG_BIG, F32)
        acc_ref[h] = jnp.zeros((MOBA_PV_ROWS, MOBA_QT), F32)

    def scores(j, h, slot):
        lanes = pl.ds((h // 2) * LANES, LANES)
        s_ref[slot * MOBA_HEADS + h] = jnp.dot(k_ref[j, :, lanes], qtb_ref[h],
                                               preferred_element_type=F32).astype(BF16)

    def attend(j, h, slot, selected, keep=None):
        s = s_ref[slot * MOBA_HEADS + h]
        if keep is not None:
            s = jnp.where(keep, s, NEG_BIG)
        m_run = m_ref[h]
        parts = s.reshape(MOBA_MAX_PARTS, MOBA_BLOCK // MOBA_MAX_PARTS, MOBA_QT)
        blk_max = jnp.max(jnp.max(parts, axis=0), axis=0, keepdims=True)
        m_new = jnp.maximum(m_run, jnp.where(selected, blk_max.astype(F32), NEG_BIG))
        shift = jnp.where(selected, m_new, -NEG_BIG).astype(BF16)
        p = jnp.exp2(s - shift)
        v_aug = jnp.concatenate([vt_ref[j, pl.ds(h * MOBA_DH, MOBA_DH), :], ones_rows], axis=0)
        pv = jnp.dot(v_aug, p, preferred_element_type=F32)
        acc_ref[h] = acc_ref[h] * jnp.exp2(m_run - m_new) + pv
        m_ref[h] = m_new

    def step(j, slot, flags):
        for pair in range(MOBA_HEADS // 2):
            scores(j + 1, 2 * pair, 1 - slot)
            scores(j + 1, 2 * pair + 1, 1 - slot)
            for h in (2 * pair, 2 * pair + 1):
                attend(j, h, slot, *flags(h))

    def past(j, slot):
        step(j, slot, lambda h: (sel_ref[h, pl.ds(j, 1), :] > 0.0,))

    def past_group(t, carry):
        for u in range(MOBA_UNROLL):
            past(MOBA_UNROLL * t + u, u % 2)
        return carry

    for h in range(MOBA_HEADS):
        scores(0, h, 0)
    lax.fori_loop(0, first_own // MOBA_UNROLL, past_group, 0)
    done = (first_own // MOBA_UNROLL) * MOBA_UNROLL
    for u in range(0, MOBA_UNROLL - MOBA_QBLOCKS, 1):

        @pl.when(first_own - done > u)
        def _():
            past(done + u, u % 2)

    for b in range(MOBA_QBLOCKS):
        is_own = q_blk == b
        keep = (q_col // MOBA_BLOCK != b) | (key_pos <= q_col - b * MOBA_BLOCK)
        flags = lambda h, b=b, is_own=is_own, keep=keep: (
            is_own | ((q_blk > b) & (sel_ref[h, pl.ds(first_own + b, 1), :] > 0.0)), keep)
        if b + 1 < MOBA_QBLOCKS:
            step(first_own + b, b % 2, flags)
        else:
            for h in range(MOBA_HEADS):
                attend(first_own + b, h, b % 2, *flags(h))

    for h in range(MOBA_HEADS):
        out_ref[h * MOBA_DH:(h + 1) * MOBA_DH, :] = acc_ref[h, 0:MOBA_DH, :] / acc_ref[h, MOBA_DH:MOBA_DH + 1, :]
    for b in range(MOBA_QBLOCKS):
        o_ref[b * MOBA_BLOCK:(b + 1) * MOBA_BLOCK, :] = out_ref[:, b * MOBA_BLOCK:(b + 1) * MOBA_BLOCK].T


def _moba(qt3, k3, vt3, kmean):
    nblk = k3.shape[0]
    seq = nblk * MOBA_BLOCK
    return pl.pallas_call(
        _moba_kernel,
        grid=(nblk // MOBA_QBLOCKS,),
        in_specs=[pl.BlockSpec((MOBA_QBLOCKS, MOBA_W, MOBA_BLOCK), lambda i: (i, 0, 0)),
                  _const_spec((nblk, MOBA_BLOCK, MOBA_W)), _const_spec((nblk, MOBA_W, MOBA_BLOCK)),
                  _const_spec((nblk, MOBA_W))],
        out_specs=pl.BlockSpec((MOBA_QT, MOBA_W), lambda i: (i, 0)),
        out_shape=jax.ShapeDtypeStruct((seq, MOBA_W), F32),
        scratch_shapes=[pltpu.VMEM((MOBA_HEADS, nblk, MOBA_QT), F32),
                        pltpu.VMEM((MOBA_HEADS, LANES, MOBA_QT), BF16),
                        pltpu.VMEM((MOBA_HEADS, 1, MOBA_QT), F32),
                        pltpu.VMEM((MOBA_HEADS, MOBA_PV_ROWS, MOBA_QT), F32),
                        pltpu.VMEM((MOBA_W, MOBA_QT), F32),
                        pltpu.VMEM((2 * MOBA_HEADS, MOBA_BLOCK, MOBA_QT), BF16)],
        compiler_params=_params("parallel"),
        name="moba",
    )(qt3, k3, vt3, kmean)


def _merge_kernel(x_ref, g_ref, gla_ref, moba_ref, loc_ref, halo_ref, wpool_ref, pscale_ref, gsgu_ref, wsgu_ref,
                  bsgu_ref, wgate_ref, wbr_ref, wout_ref, y_ref, zpad_ref):
    x = x_ref[...]
    h = _rms(x, g_ref[...]).astype(BF16)

    def gated(b, branch):
        gate = _sigmoid(jnp.dot(h, wgate_ref[:, b * D_MODEL:(b + 1) * D_MODEL], preferred_element_type=F32))
        return gate * jnp.dot(branch.astype(BF16), wbr_ref[b], preferred_element_type=F32)

    merged = gated(0, gla_ref[...]) + gated(3, moba_ref[...])
    pool, sgu = _local_mixers(loc_ref, halo_ref, wpool_ref, pscale_ref, gsgu_ref, wsgu_ref, bsgu_ref, zpad_ref)
    merged = merged + gated(1, pool) + gated(2, sgu)
    y_ref[...] = x + jnp.dot(merged.astype(BF16), wout_ref[...], preferred_element_type=F32)


def _merge(x, g, o_gla, o_moba, loc, wpool_bd, pscale, gsgu, wsgu, bsgu_full, wgate, wbr, wout):
    seq = x.shape[0]
    per = ROW_TILE // HALO
    row = lambda w: pl.BlockSpec((ROW_TILE, w), lambda i: (i, 0))
    return pl.pallas_call(
        _merge_kernel,
        grid=(seq // ROW_TILE,),
        in_specs=[row(D_MODEL), _const_spec((1, D_MODEL)), row(BRANCH_W), row(BRANCH_W), row(LOC_W),
                  pl.BlockSpec((HALO, BRANCH_W), lambda i: (jnp.maximum(i * per - 1, 0), 0)),
                  _const_spec((BRANCH_W, BRANCH_W)), _const_spec((1, BRANCH_W)), _const_spec((1, BRANCH_W)),
                  _const_spec((SGU_GROUPS, SGU_CHUNK, SGU_CHUNK)), _const_spec((SGU_CHUNK, BRANCH_W)),
                  _const_spec((D_MODEL, N_BRANCH * D_MODEL)), _const_spec((N_BRANCH, BRANCH_W, D_MODEL)),
                  _const_spec((D_MODEL, D_MODEL))],
        out_specs=row(D_MODEL),
        out_shape=jax.ShapeDtypeStruct((seq, D_MODEL), F32),
        scratch_shapes=[pltpu.VMEM((HALO + ROW_TILE, BRANCH_W), F32)],
        compiler_params=_params("parallel"),
        name="merge",
    )(x, g, o_gla, o_moba, loc, loc, wpool_bd, pscale, gsgu, wsgu, bsgu_full, wgate, wbr, wout)


def _ffn_kernel(x_ref, halo_ref, g_ref, wup_ref, cw_ref, cb_ref, wdown_ref, gfin_ref, y_ref,
                h_ref, acc_ref, up_ref, act_ref, *, final_norm):
    rows = x_ref.shape[0]
    x = x_ref[...]
    h_halo = _rms(halo_ref[...], g_ref[...])
    h_ref[0:HALO, :] = jnp.where(pl.program_id(0) > 0, h_halo, 0.0).astype(BF16)
    h_ref[HALO:HALO + rows, :] = _rms(x, g_ref[...]).astype(BF16)
    n_chunks = D_FF // FF_CHUNK

    def up_proj(c, slot):
        for half in range(2):
            cols = pl.ds(half * D_FF + c * FF_CHUNK, FF_CHUNK)
            up_ref[slot, half] = jnp.dot(h_ref[...], wup_ref[:, cols], preferred_element_type=F32)

    def conv(c, slot, half):
        cols = pl.ds(half * D_FF + c * FF_CHUNK, FF_CHUNK)
        cw = cw_ref[:, cols]
        return (up_ref[slot, half, HALO - 2:HALO - 2 + rows, :] * cw[0:1] +
                up_ref[slot, half, HALO - 1:HALO - 1 + rows, :] * cw[1:2] +
                up_ref[slot, half, HALO:HALO + rows, :] * cw[2:3] + cb_ref[:, cols])

    acc_ref[...] = x
    up_proj(0, 0)
    for c in range(n_chunks + 1):
        if c + 1 < n_chunks:
            up_proj(c + 1, (c + 1) % 2)
        if c >= 1:
            acc_ref[...] += jnp.dot(act_ref[(c - 1) % 2], wdown_ref[(c - 1) * FF_CHUNK:c * FF_CHUNK, :],
                                    preferred_element_type=F32)
        if c < n_chunks:
            a = conv(c, c % 2, 0)
            b = conv(c, c % 2, 1)
            act_ref[c % 2] = (a * _sigmoid(a) * b).astype(BF16)
    y = acc_ref[...]
    y_ref[...] = _rms(y, gfin_ref[...]) if final_norm else y


def _ffn(x, g, wup, cw, cb, wdown, gfin, final_norm):
    seq = x.shape[0]
    per = FFN_TILE // HALO
    row = pl.BlockSpec((FFN_TILE, D_MODEL), lambda i: (i, 0))
    return pl.pallas_call(
        functools.partial(_ffn_kernel, final_norm=final_norm),
        grid=(seq // FFN_TILE,),
        in_specs=[row, pl.BlockSpec((HALO, D_MODEL), lambda i: (jnp.maximum(i * per - 1, 0), 0)),
                  _const_spec((1, D_MODEL)), _const_spec((D_MODEL, 2 * D_FF)), _const_spec((3, 2 * D_FF)),
                  _const_spec((1, 2 * D_FF)), _const_spec((D_FF, D_MODEL)), _const_spec((1, D_MODEL))],
        out_specs=row,
        out_shape=jax.ShapeDtypeStruct((seq, D_MODEL), F32),
        scratch_shapes=[pltpu.VMEM((HALO + FFN_TILE, D_MODEL), BF16), pltpu.VMEM((FFN_TILE, D_MODEL), F32),
                        pltpu.VMEM((2, 2, HALO + FFN_TILE, FF_CHUNK), F32),
                        pltpu.VMEM((2, FFN_TILE, FF_CHUNK), BF16)],
        compiler_params=_params("parallel"),
        name="conv_ffn",
    )(x, x, g, wup, cw, cb, wdown, gfin)


def _pack_w_in_kernel(wt_ref, o_ref):
    n_gate_end = C_GA + GLA_RANK
    d_model = wt_ref.shape[2]
    for c in range(C_END // LANES):
        if (c + 1) * LANES <= C_GA:
            cols = wt_ref[0, c * LANES:(c + 1) * LANES, :]
        elif c * LANES == C_GA:
            cols = jnp.concatenate([wt_ref[0, C_GA:n_gate_end, :],
                                    jnp.zeros((LANES - GLA_RANK, d_model), F32)], axis=0)
        else:
            src = n_gate_end + c * LANES - C_LOC
            cols = wt_ref[0, src:src + LANES, :]
        o_ref[:, c * LANES:(c + 1) * LANES] = cols.T.astype(BF16)


def _pack_w_in(w_in, layer):
    _, rows, cols = w_in.shape
    return pl.pallas_call(
        _pack_w_in_kernel,
        grid=(1,),
        in_specs=[pl.BlockSpec((1, cols, rows), lambda i: (layer, 0, 0))],
        out_specs=pl.BlockSpec((rows, C_END), lambda i: (0, 0)),
        out_shape=jax.ShapeDtypeStruct((rows, C_END), BF16),
        compiler_params=_params("arbitrary"),
        name="pack_w_in",
    )(jnp.swapaxes(w_in, 1, 2))


def _cast_kernel(w_ref, o_ref):
    o_ref[...] = w_ref[0].astype(o_ref.dtype)


def _layer_bf16(w, layer):
    _, rows, cols = w.shape
    block = min(rows, CAST_ROWS)
    return pl.pallas_call(
        _cast_kernel,
        grid=(rows // block,),
        in_specs=[pl.BlockSpec((1, block, cols), lambda i: (layer, i, 0))],
        out_specs=pl.BlockSpec((block, cols), lambda i: (i, 0)),
        out_shape=jax.ShapeDtypeStruct((rows, cols), BF16),
        compiler_params=_params("parallel"),
        name="cast_bf16",
    )(w)


def _block_diag(w):
    g, a, b = w.shape
    eye = jnp.eye(g, dtype=w.dtype)
    return (eye[:, None, :, None] * w[:, :, None, :]).reshape(g * a, g * b)


def kernel(x, positions, g_mix, w_in, w_gla_a, b_gla_a, g_gla_norm, w_pool, pool_scale, g_sgu, w_sgu, b_sgu,
           w_gate, w_branch, w_out, g_ffn, w_up, conv_w, conv_b, w_down, g_final):
    bsz, seq, d = x.shape
    assert bsz == 1 and d == D_MODEL and seq % FFN_TILE == 0 and seq % ROW_TILE == 0 and seq % GLA_TILE == 0
    depth = w_in.shape[0]
    xs = x[0]
    pos = positions.reshape(seq, 1)
    half = MOBA_DH // 2
    inv_freq = ROPE_THETA ** (-jnp.arange(half, dtype=F32) / half)
    invf = jnp.tile(inv_freq, LANES // half)[None, :]
    row = lambda a: a.reshape(1, -1)

    for l in range(depth):
        wa2 = jnp.concatenate([w_gla_a[l], jnp.zeros((LANES - GLA_RANK, GLA_QK), F32)], axis=0)
        gla_in, loc, qt3, k, vt3, kmean = _inproj(xs, pos, row(g_mix[l]), _pack_w_in(w_in, l), wa2,
                                                  row(b_gla_a[l]), invf)
        o_gla = _gla(gla_in, row(jnp.tile(g_gla_norm[l], GLA_HEADS)))
        o_moba = _moba(qt3, k.reshape(seq // MOBA_BLOCK, MOBA_BLOCK, MOBA_W), vt3,
                       kmean.reshape(seq // MOBA_BLOCK, MOBA_W))
        xs = _merge(xs, row(g_mix[l]), o_gla, o_moba, loc, _block_diag(w_pool[l]).astype(BF16),
                    row(pool_scale[l]), row(g_sgu[l]), w_sgu[l], jnp.repeat(b_sgu[l].T, SGU_GW, axis=1),
                    _layer_bf16(w_gate, l),
                    _layer_bf16(w_branch.reshape(depth, N_BRANCH * BRANCH_W, D_MODEL), l).reshape(
                        N_BRANCH, BRANCH_W, D_MODEL),
                    _layer_bf16(w_out, l))
        xs = _ffn(xs, row(g_ffn[l]), _layer_bf16(w_up, l), conv_w[l], row(conv_b[l]), _layer_bf16(w_down, l),
                  row(g_final), final_norm=(l == depth - 1))
    return xs[None]
```

```python
import functools

import numpy as np
import jax
import jax.numpy as jnp
from jax import lax
from jax.experimental import pallas as pl
from jax.experimental.pallas import tpu as pltpu

F32 = jnp.float32
BF16 = jnp.bfloat16
HIGHEST = lax.Precision.HIGHEST

D_MODEL = 1024
N_BRANCH = 4
BRANCH_W = 256
GLA_HEADS, GLA_DK, GLA_DV, GLA_RANK, GLA_TAU = 4, 32, 64, 16, 16.0
GLA_QK = GLA_HEADS * GLA_DK
GLA_V = GLA_HEADS * GLA_DV
POOL_WINDOWS = (2, 4, 8, 16)
POOL_GW = BRANCH_W // len(POOL_WINDOWS)
SGU_CHUNK, SGU_GROUPS = 128, 4
SGU_GW = BRANCH_W // SGU_GROUPS
MOBA_HEADS, MOBA_DH, MOBA_BLOCK, MOBA_TOPK = 4, 64, 256, 3
MOBA_W = MOBA_HEADS * MOBA_DH
MOBA_PV_ROWS = MOBA_DH + 16
MOBA_UNROLL = 4
MOBA_MAX_PARTS = 8
ROPE_THETA = 10000.0
D_FF = 2816
EPS = 1e-6

LANES = 128
VMEM_LIMIT_BYTES = 56 * 1024 * 1024

ROW_TILE = 512
FFN_TILE = 512
GLA_TILE = 256
GLA_SUB = 16
HALO = 16
FF_CHUNK = 256
CAST_ROWS = 256
NEG_BIG = -1e30
LOG2_E = float(np.log2(np.e))

C_GLA = 0
C_GA = C_GLA + 2 * GLA_QK + 2 * GLA_V
C_LOC = C_GA + LANES
C_MQ = C_LOC + BRANCH_W + 2 * BRANCH_W
C_MK = C_MQ + MOBA_W
C_MV = C_MK + MOBA_W
C_END = C_MV + MOBA_W
GLA_IN_W = C_GA + GLA_QK
LOC_W = 3 * BRANCH_W


def _rms(x, g):
    return x * lax.rsqrt(jnp.mean(x * x, axis=-1, keepdims=True) + EPS) * g


def _sigmoid(x):
    return 1.0 / (1.0 + jnp.exp(-x))


def _params(*sem):
    return pltpu.CompilerParams(dimension_semantics=sem, vmem_limit_bytes=VMEM_LIMIT_BYTES)


def _const_spec(shape):
    return pl.BlockSpec(shape, lambda i: (0,) * len(shape), pipeline_mode=pl.Buffered(1))


def _rope_tile(x, cos, sin, first_half):
    partner = jnp.where(first_half, -pltpu.roll(x, LANES - MOBA_DH // 2, axis=1),
                        pltpu.roll(x, MOBA_DH // 2, axis=1))
    return x * cos + partner * sin


def _inproj_kernel(x_ref, pos_ref, g_ref, w_ref, wa2_ref, ba_ref, invf_ref,
                   gla_ref, loc_ref, qt_ref, k_ref, vt_ref, kmean_ref):
    rows = x_ref.shape[0]
    h = _rms(x_ref[...], g_ref[...]).astype(BF16)
    proj = jnp.dot(h, w_ref[...], preferred_element_type=F32)
    proj_m = proj[:, C_MQ:C_END]
    loc_ref[...] = proj[:, C_LOC:C_MQ]

    ang = pos_ref[...].astype(F32) * invf_ref[...]
    cos, sin = jnp.cos(ang), jnp.sin(ang)
    lane = lax.broadcasted_iota(jnp.int32, (rows, LANES), 1)
    first_half = (lane % MOBA_DH) < (MOBA_DH // 2)

    def roped(c0):
        return jnp.concatenate(
            [_rope_tile(proj_m[:, c0 + t * LANES:c0 + (t + 1) * LANES], cos, sin, first_half)
             for t in range(MOBA_W // LANES)], axis=1)

    q = roped(0) * (MOBA_DH ** -0.5)
    k = roped(MOBA_W)
    v = proj_m[:, 2 * MOBA_W:]

    proj_g = proj[:, C_GLA:C_LOC]
    gla_ref[:, 0:C_GA] = proj_g[:, 0:C_GA]
    logit = jnp.dot(proj_g[:, C_GA:C_LOC], wa2_ref[...], precision=HIGHEST,
                    preferred_element_type=F32) + ba_ref[...]
    log_sig = jnp.minimum(logit, 0.0) - jnp.log(1.0 + jnp.exp(-jnp.abs(logit)))
    gla_ref[:, C_GA:GLA_IN_W] = log_sig * (1.0 / GLA_TAU)

    k_ref[...] = k.astype(BF16)
    for r in range(rows // MOBA_BLOCK):
        sl = slice(r * MOBA_BLOCK, (r + 1) * MOBA_BLOCK)
        qt_ref[r] = q[sl].T
        vt_ref[r] = v[sl].T.astype(BF16)
        kmean_ref[0, r:r + 1, :] = jnp.mean(k[sl], axis=0, keepdims=True)


def _inproj(x, pos, g, w_cat, wa2, ba, invf):
    seq = x.shape[0]
    nblk = ROW_TILE // MOBA_BLOCK
    row = lambda w: pl.BlockSpec((ROW_TILE, w), lambda i: (i, 0))
    blk3 = pl.BlockSpec((nblk, MOBA_W, MOBA_BLOCK), lambda i: (i, 0, 0))
    return pl.pallas_call(
        _inproj_kernel,
        grid=(seq // ROW_TILE,),
        in_specs=[row(D_MODEL), row(1), _const_spec((1, D_MODEL)), _const_spec((D_MODEL, C_END)),
                  _const_spec((LANES, GLA_QK)), _const_spec((1, GLA_QK)), _const_spec((1, LANES))],
        out_specs=[row(GLA_IN_W), row(LOC_W), blk3, row(MOBA_W), blk3,
                   pl.BlockSpec((1, nblk, MOBA_W), lambda i: (i, 0, 0))],
        out_shape=[jax.ShapeDtypeStruct((seq, GLA_IN_W), F32),
                   jax.ShapeDtypeStruct((seq, LOC_W), F32),
                   jax.ShapeDtypeStruct((seq // MOBA_BLOCK, MOBA_W, MOBA_BLOCK), F32),
                   jax.ShapeDtypeStruct((seq, MOBA_W), BF16),
                   jax.ShapeDtypeStruct((seq // MOBA_BLOCK, MOBA_W, MOBA_BLOCK), BF16),
                   jax.ShapeDtypeStruct((seq // ROW_TILE, nblk, MOBA_W), F32)],
        compiler_params=_params("parallel"),
        name="inproj",
    )(x, pos, g, w_cat, wa2, ba, invf)


def _gla_kernel(gin_ref, gnorm_ref, out_ref, state_ref, kpad_ref, vpad_ref, bpad_ref,
                qb_ref, ke_ref, eb_ref, o_ref, kv_ref):
    rows = gin_ref.shape[0]

    @pl.when(pl.program_id(0) == 0)
    def _():
        state_ref[...] = jnp.zeros_like(state_ref)
        kpad_ref[0:HALO, :] = jnp.zeros((HALO, GLA_QK), F32)
        vpad_ref[0:HALO, :] = jnp.zeros((HALO, GLA_V), F32)
        bpad_ref[0:HALO, :] = jnp.zeros((HALO, GLA_QK), F32)

    q = gin_ref[:, 0:GLA_QK] * (GLA_DK ** -0.5)
    k = gin_ref[:, GLA_QK:2 * GLA_QK]
    v = gin_ref[:, 2 * GLA_QK:2 * GLA_QK + GLA_V]
    la = gin_ref[:, C_GA:GLA_IN_W]

    rowmod = lax.broadcasted_iota(jnp.int32, (rows, GLA_QK), 0) % GLA_SUB
    b = la
    shift = 1
    while shift < GLA_SUB:
        b = b + jnp.where(rowmod >= shift, pltpu.roll(b, shift, axis=0), 0.0)
        shift *= 2
    b_last = jnp.broadcast_to(b.reshape(rows // GLA_SUB, GLA_SUB, GLA_QK)[:, GLA_SUB - 1:GLA_SUB, :],
                              (rows // GLA_SUB, GLA_SUB, GLA_QK)).reshape(rows, GLA_QK)
    eb = jnp.exp(b)
    qb_ref[...] = (q * eb).astype(BF16)
    ke_ref[...] = (k * jnp.exp(b_last - b)).astype(BF16)
    eb_ref[...] = eb

    kpad_ref[HALO:HALO + rows, :] = k
    vpad_ref[HALO:HALO + rows, :] = v
    bpad_ref[HALO:HALO + rows, :] = b

    head_sum = (lax.broadcasted_iota(jnp.int32, (GLA_QK, GLA_V), 0) // GLA_DK ==
                lax.broadcasted_iota(jnp.int32, (GLA_QK, GLA_V), 1) // GLA_DV).astype(BF16)
    o = jnp.zeros((rows, GLA_V), F32)
    for d in range(GLA_SUB):
        lo = HALO - d
        kd = kpad_ref[lo:lo + rows, :]
        bd = bpad_ref[lo:lo + rows, :]
        vd = vpad_ref[lo:lo + rows, :]
        decay = jnp.exp(jnp.where(rowmod >= d, b - bd, NEG_BIG))
        p = (q * kd * decay).astype(BF16)
        o = o + jnp.dot(p, head_sum, preferred_element_type=F32) * vd
    o_ref[...] = o

    state_mask = (lax.broadcasted_iota(jnp.int32, (GLA_V, GLA_QK), 0) // GLA_DV ==
                  lax.broadcasted_iota(jnp.int32, (GLA_V, GLA_QK), 1) // GLA_DK)

    steps = rows // GLA_SUB
    for s in range(steps):
        blk = slice(s * GLA_SUB, (s + 1) * GLA_SUB)
        v_b = gin_ref[blk, 2 * GLA_QK:2 * GLA_QK + GLA_V].astype(BF16)
        kv = lax.dot_general(v_b, ke_ref[blk, :], (((0,), (0,)), ((), ())),
                             preferred_element_type=F32)
        kv_ref[s] = jnp.where(state_mask, kv, 0.0)

    state = state_ref[...]
    for s in range(steps):
        blk = slice(s * GLA_SUB, (s + 1) * GLA_SUB)
        o_ref[blk, :] += lax.dot_general(qb_ref[blk, :], state.astype(BF16),
                                         (((1,), (1,)), ((), ())), preferred_element_type=F32)
        state = state * eb_ref[(s + 1) * GLA_SUB - 1:(s + 1) * GLA_SUB, :] + kv_ref[s]
    state_ref[...] = state

    o = o_ref[...]
    head_mean = (lax.broadcasted_iota(jnp.int32, (GLA_V, GLA_V), 0) // GLA_DV ==
                 lax.broadcasted_iota(jnp.int32, (GLA_V, GLA_V), 1) // GLA_DV).astype(F32) * (1.0 / GLA_DV)
    ms = jnp.dot(o * o, head_mean, precision=HIGHEST, preferred_element_type=F32)
    r = gin_ref[:, 2 * GLA_QK + GLA_V:C_GA]
    out_ref[...] = o * lax.rsqrt(ms + EPS) * gnorm_ref[...] * (r * _sigmoid(r))


def _gla(gla_in, gnorm):
    seq = gla_in.shape[0]
    return pl.pallas_call(
        _gla_kernel,
        grid=(seq // GLA_TILE,),
        in_specs=[pl.BlockSpec((GLA_TILE, GLA_IN_W), lambda i: (i, 0)), _const_spec((1, GLA_V))],
        out_specs=pl.BlockSpec((GLA_TILE, GLA_V), lambda i: (i, 0)),
        out_shape=jax.ShapeDtypeStruct((seq, GLA_V), F32),
        scratch_shapes=[pltpu.VMEM((GLA_V, GLA_QK), F32),
                        pltpu.VMEM((HALO + GLA_TILE, GLA_QK), F32),
                        pltpu.VMEM((HALO + GLA_TILE, GLA_V), F32),
                        pltpu.VMEM((HALO + GLA_TILE, GLA_QK), F32),
                        pltpu.VMEM((GLA_TILE, GLA_QK), BF16),
                        pltpu.VMEM((GLA_TILE, GLA_QK), BF16),
                        pltpu.VMEM((GLA_TILE, GLA_QK), F32),
                        pltpu.VMEM((GLA_TILE, GLA_V), F32),
                        pltpu.VMEM((GLA_TILE // GLA_SUB, GLA_V, GLA_QK), F32)],
        compiler_params=_params("arbitrary"),
        name="gla",
    )(gla_in, gnorm)


def _local_mixers(loc_ref, halo_ref, wpool_ref, pscale_ref, gsgu_ref, wsgu_ref, bsgu_ref, zpad_ref):
    rows = loc_ref.shape[0]
    i = pl.program_id(0)

    z = loc_ref[:, 0:BRANCH_W]
    zpad_ref[0:HALO, :] = jnp.where(i > 0, halo_ref[...], 0.0)
    zpad_ref[HALO:HALO + rows, :] = z
    t = i * rows + lax.broadcasted_iota(jnp.int32, (rows, BRANCH_W), 0)
    group = lax.broadcasted_iota(jnp.int32, (rows, BRANCH_W), 1) // POOL_GW
    acc = zpad_ref[...]
    mean = jnp.zeros_like(z)
    width = 1
    for g, w in enumerate(POOL_WINDOWS):
        while width < w:
            acc = acc + pltpu.roll(acc, width, axis=0)
            width *= 2
        count = jnp.minimum(t + 1, w).astype(F32)
        mean = jnp.where(group == g, acc[HALO:] / count, mean)
    mixed = (mean - z).astype(BF16)
    pool = jnp.dot(mixed, wpool_ref[...], preferred_element_type=F32) * pscale_ref[...]

    zs = loc_ref[:, BRANCH_W:LOC_W]
    gz = 0.5 * zs * (1.0 + jnp.tanh(np.sqrt(2.0 / np.pi).astype(np.float32) * (zs + 0.044715 * (zs * zs * zs))))
    u = gz[:, 0:BRANCH_W]
    vn = _rms(gz[:, BRANCH_W:], gsgu_ref[...]).astype(BF16)
    tril = (lax.broadcasted_iota(jnp.int32, (SGU_CHUNK, SGU_CHUNK), 0) >=
            lax.broadcasted_iota(jnp.int32, (SGU_CHUNK, SGU_CHUNK), 1))
    w_all = jnp.concatenate([jnp.where(tril, wsgu_ref[g], 0.0).astype(BF16) for g in range(SGU_GROUPS)], axis=1)
    cgroup = lax.broadcasted_iota(jnp.int32, (SGU_CHUNK, BRANCH_W), 1) // SGU_GW
    sgu = []
    for c in range(rows // SGU_CHUNK):
        sl = slice(c * SGU_CHUNK, (c + 1) * SGU_CHUNK)
        v_grp = jnp.concatenate([jnp.where(cgroup == g, vn[sl], 0.0) for g in range(SGU_GROUPS)], axis=0)
        sv = jnp.dot(w_all, v_grp, preferred_element_type=F32) + bsgu_ref[...]
        sgu.append(u[sl] * sv)
    return pool, jnp.concatenate(sgu, axis=0)


def _moba_kernel(qt_ref, k_ref, vt_ref, kmean_ref, o_ref, sel_ref, qtb_ref, m_ref, acc_ref, out_ref, s_ref):
    own = pl.program_id(0)
    nblk = k_ref.shape[0]
    blk_id = lax.broadcasted_iota(jnp.int32, (nblk, MOBA_BLOCK), 0)
    pair_row = lax.broadcasted_iota(jnp.int32, (LANES, MOBA_BLOCK), 0) // MOBA_DH
    key_pos = lax.broadcasted_iota(jnp.int32, (MOBA_BLOCK, MOBA_BLOCK), 0)
    q_pos = lax.broadcasted_iota(jnp.int32, (MOBA_BLOCK, MOBA_BLOCK), 1)
    ones_rows = (lax.broadcasted_iota(jnp.int32, (MOBA_PV_ROWS - MOBA_DH, MOBA_BLOCK), 0) == 0).astype(BF16)

    for h in range(MOBA_HEADS):
        lanes = pl.ds((h // 2) * LANES, LANES)
        qt = jnp.where(pair_row == (h % 2), qt_ref[0, (h // 2) * LANES:(h // 2 + 1) * LANES, :], 0.0)
        qtb_ref[h] = (qt * LOG2_E).astype(BF16)

        gate = jnp.dot(kmean_ref[:, lanes], qt, precision=HIGHEST, preferred_element_type=F32)
        gate = jnp.where(blk_id < own, gate, -jnp.inf)
        sel = jnp.zeros((nblk, MOBA_BLOCK), F32)
        for _ in range(MOBA_TOPK):
            best = jnp.max(gate, axis=0, keepdims=True)
            first = jnp.min(jnp.where(gate == best, blk_id, nblk), axis=0, keepdims=True)
            pick = (blk_id == first) & (best > -jnp.inf)
            sel = jnp.where(pick, 1.0, sel)
            gate = jnp.where(pick, -jnp.inf, gate)
        sel_ref[h] = sel
        m_ref[h] = jnp.full((1, MOBA_BLOCK), NEG_BIG, F32)
        acc_ref[h] = jnp.zeros((MOBA_PV_ROWS, MOBA_BLOCK), F32)

    def scores(j, h, slot):
        lanes = pl.ds((h // 2) * LANES, LANES)
        s_ref[slot * MOBA_HEADS + h, 0:MOBA_BLOCK, :] = jnp.dot(k_ref[j, :, lanes], qtb_ref[h],
                                                                preferred_element_type=F32).astype(BF16)

    def attend(j, h, slot, selected, causal):
        s = s_ref[slot * MOBA_HEADS + h, 0:MOBA_BLOCK, :]
        if causal:
            s = jnp.where(key_pos <= q_pos, s, NEG_BIG)
        m_run = m_ref[h]
        parts = s.reshape(MOBA_MAX_PARTS, MOBA_BLOCK // MOBA_MAX_PARTS, MOBA_BLOCK)
        blk_max = jnp.max(jnp.max(parts, axis=0), axis=0, keepdims=True)
        m_new = jnp.maximum(m_run, jnp.where(selected, blk_max.astype(F32), NEG_BIG))
        shift = jnp.where(selected, m_new, -NEG_BIG).astype(BF16)
        p = jnp.exp2(s - shift)
        v_aug = jnp.concatenate([vt_ref[j, pl.ds(h * MOBA_DH, MOBA_DH), :], ones_rows], axis=0)
        pv = jnp.dot(v_aug, p, preferred_element_type=F32)
        acc_ref[h] = acc_ref[h] * jnp.exp2(m_run - m_new) + pv
        m_ref[h] = m_new

    def past(j, slot):
        for pair in range(MOBA_HEADS // 2):
            scores(j + 1, 2 * pair, 1 - slot)
            scores(j + 1, 2 * pair + 1, 1 - slot)
            for h in (2 * pair, 2 * pair + 1):
                attend(j, h, slot, sel_ref[h, pl.ds(j, 1), :] > 0.0, False)

    def past_group(t, carry):
        for u in range(MOBA_UNROLL):
            past(MOBA_UNROLL * t + u, u % 2)
        return carry

    def own_block(slot):
        everyone = jnp.full((1, MOBA_BLOCK), True)
        for h in range(MOBA_HEADS):
            attend(own, h, slot, everyone, True)

    for h in range(MOBA_HEADS):
        scores(0, h, 0)
    lax.fori_loop(0, own // MOBA_UNROLL, past_group, 0)
    done = (own // MOBA_UNROLL) * MOBA_UNROLL
    for u in range(MOBA_UNROLL - 1):

        @pl.when(own - done > u)
        def _():
            past(done + u, u % 2)

    @pl.when(own % 2 == 0)
    def _():
        own_block(0)

    @pl.when(own % 2 == 1)
    def _():
        own_block(1)

    for h in range(MOBA_HEADS):
        out_ref[h * MOBA_DH:(h + 1) * MOBA_DH, :] = acc_ref[h, 0:MOBA_DH, :] / acc_ref[h, MOBA_DH:MOBA_DH + 1, :]
    o_ref[...] = out_ref[...].T


def _moba(qt3, k3, vt3, kmean):
    nblk = k3.shape[0]
    seq = nblk * MOBA_BLOCK
    return pl.pallas_call(
        _moba_kernel,
        grid=(nblk,),
        in_specs=[pl.BlockSpec((1, MOBA_W, MOBA_BLOCK), lambda i: (i, 0, 0)),
                  _const_spec((nblk, MOBA_BLOCK, MOBA_W)), _const_spec((nblk, MOBA_W, MOBA_BLOCK)),
                  _const_spec((nblk, MOBA_W))],
        out_specs=pl.BlockSpec((MOBA_BLOCK, MOBA_W), lambda i: (i, 0)),
        out_shape=jax.ShapeDtypeStruct((seq, MOBA_W), F32),
        scratch_shapes=[pltpu.VMEM((MOBA_HEADS, nblk, MOBA_BLOCK), F32),
                        pltpu.VMEM((MOBA_HEADS, LANES, MOBA_BLOCK), BF16),
                        pltpu.VMEM((MOBA_HEADS, 1, MOBA_BLOCK), F32),
                        pltpu.VMEM((MOBA_HEADS, MOBA_PV_ROWS, MOBA_BLOCK), F32),
                        pltpu.VMEM((MOBA_W, MOBA_BLOCK), F32),
                        pltpu.VMEM((2 * MOBA_HEADS, MOBA_BLOCK + 16, MOBA_BLOCK), BF16)],
        compiler_params=_params("parallel"),
        name="moba",
    )(qt3, k3, vt3, kmean)


def _merge_kernel(x_ref, g_ref, gla_ref, moba_ref, loc_ref, halo_ref, wpool_ref, pscale_ref, gsgu_ref, wsgu_ref,
                  bsgu_ref, wgate_ref, wbr_ref, wout_ref, y_ref, zpad_ref):
    x = x_ref[...]
    h = _rms(x, g_ref[...]).astype(BF16)

    def gated(b, branch):
        gate = _sigmoid(jnp.dot(h, wgate_ref[:, b * D_MODEL:(b + 1) * D_MODEL], preferred_element_type=F32))
        return gate * jnp.dot(branch.astype(BF16), wbr_ref[b], preferred_element_type=F32)

    merged = gated(0, gla_ref[...]) + gated(3, moba_ref[...])
    pool, sgu = _local_mixers(loc_ref, halo_ref, wpool_ref, pscale_ref, gsgu_ref, wsgu_ref, bsgu_ref, zpad_ref)
    merged = merged + gated(1, pool) + gated(2, sgu)
    y_ref[...] = x + jnp.dot(merged.astype(BF16), wout_ref[...], preferred_element_type=F32)


def _merge(x, g, o_gla, o_moba, loc, wpool_bd, pscale, gsgu, wsgu, bsgu_full, wgate, wbr, wout):
    seq = x.shape[0]
    per = ROW_TILE // HALO
    row = lambda w: pl.BlockSpec((ROW_TILE, w), lambda i: (i, 0))
    return pl.pallas_call(
        _merge_kernel,
        grid=(seq // ROW_TILE,),
        in_specs=[row(D_MODEL), _const_spec((1, D_MODEL)), row(BRANCH_W), row(BRANCH_W), row(LOC_W),
                  pl.BlockSpec((HALO, BRANCH_W), lambda i: (jnp.maximum(i * per - 1, 0), 0)),
                  _const_spec((BRANCH_W, BRANCH_W)), _const_spec((1, BRANCH_W)), _const_spec((1, BRANCH_W)),
                  _const_spec((SGU_GROUPS, SGU_CHUNK, SGU_CHUNK)), _const_spec((SGU_CHUNK, BRANCH_W)),
                  _const_spec((D_MODEL, N_BRANCH * D_MODEL)), _const_spec((N_BRANCH, BRANCH_W, D_MODEL)),
                  _const_spec((D_MODEL, D_MODEL))],
        out_specs=row(D_MODEL),
        out_shape=jax.ShapeDtypeStruct((seq, D_MODEL), F32),
        scratch_shapes=[pltpu.VMEM((HALO + ROW_TILE, BRANCH_W), F32)],
        compiler_params=_params("parallel"),
        name="merge",
    )(x, g, o_gla, o_moba, loc, loc, wpool_bd, pscale, gsgu, wsgu, bsgu_full, wgate, wbr, wout)


def _ffn_kernel(x_ref, halo_ref, g_ref, wup_ref, cw_ref, cb_ref, wdown_ref, gfin_ref, y_ref,
                h_ref, acc_ref, up_ref, act_ref, *, final_norm):
    rows = x_ref.shape[0]
    x = x_ref[...]
    h_halo = _rms(halo_ref[...], g_ref[...])
    h_ref[0:HALO, :] = jnp.where(pl.program_id(0) > 0, h_halo, 0.0).astype(BF16)
    h_ref[HALO:HALO + rows, :] = _rms(x, g_ref[...]).astype(BF16)
    n_chunks = D_FF // FF_CHUNK

    def up_proj(c, slot):
        for half in range(2):
            cols = pl.ds(half * D_FF + c * FF_CHUNK, FF_CHUNK)
            up_ref[slot, half] = jnp.dot(h_ref[...], wup_ref[:, cols], preferred_element_type=F32)

    def conv(c, slot, half):
        cols = pl.ds(half * D_FF + c * FF_CHUNK, FF_CHUNK)
        cw = cw_ref[:, cols]
        return (up_ref[slot, half, HALO - 2:HALO - 2 + rows, :] * cw[0:1] +
                up_ref[slot, half, HALO - 1:HALO - 1 + rows, :] * cw[1:2] +
                up_ref[slot, half, HALO:HALO + rows, :] * cw[2:3] + cb_ref[:, cols])

    acc_ref[...] = x
    up_proj(0, 0)
    for c in range(n_chunks + 1):
        if c + 1 < n_chunks:
            up_proj(c + 1, (c + 1) % 2)
        if c >= 1:
            acc_ref[...] += jnp.dot(act_ref[(c - 1) % 2], wdown_ref[(c - 1) * FF_CHUNK:c * FF_CHUNK, :],
                                    preferred_element_type=F32)
        if c < n_chunks:
            a = conv(c, c % 2, 0)
            b = conv(c, c % 2, 1)
            act_ref[c % 2] = (a * _sigmoid(a) * b).astype(BF16)
    y = acc_ref[...]
    y_ref[...] = _rms(y, gfin_ref[...]) if final_norm else y


def _ffn(x, g, wup, cw, cb, wdown, gfin, final_norm):
    seq = x.shape[0]
    per = FFN_TILE // HALO
    row = pl.BlockSpec((FFN_TILE, D_MODEL), lambda i: (i, 0))
    return pl.pallas_call(
        functools.partial(_ffn_kernel, final_norm=final_norm),
        grid=(seq // FFN_TILE,),
        in_specs=[row, pl.BlockSpec((HALO, D_MODEL), lambda i: (jnp.maximum(i * per - 1, 0), 0)),
                  _const_spec((1, D_MODEL)), _const_spec((D_MODEL, 2 * D_FF)), _const_spec((3, 2 * D_FF)),
                  _const_spec((1, 2 * D_FF)), _const_spec((D_FF, D_MODEL)), _const_spec((1, D_MODEL))],
        out_specs=row,
        out_shape=jax.ShapeDtypeStruct((seq, D_MODEL), F32),
        scratch_shapes=[pltpu.VMEM((HALO + FFN_TILE, D_MODEL), BF16), pltpu.VMEM((FFN_TILE, D_MODEL), F32),
                        pltpu.VMEM((2, 2, HALO + FFN_TILE, FF_CHUNK), F32),
                        pltpu.VMEM((2, FFN_TILE, FF_CHUNK), BF16)],
        compiler_params=_params("parallel"),
        name="conv_ffn",
    )(x, x, g, wup, cw, cb, wdown, gfin)


def _pack_w_in_kernel(wt_ref, o_ref):
    n_gate_end = C_GA + GLA_RANK
    d_model = wt_ref.shape[2]
    for c in range(C_END // LANES):
        if (c + 1) * LANES <= C_GA:
            cols = wt_ref[0, c * LANES:(c + 1) * LANES, :]
        elif c * LANES == C_GA:
            cols = jnp.concatenate([wt_ref[0, C_GA:n_gate_end, :],
                                    jnp.zeros((LANES - GLA_RANK, d_model), F32)], axis=0)
        else:
            src = n_gate_end + c * LANES - C_LOC
            cols = wt_ref[0, src:src + LANES, :]
        o_ref[:, c * LANES:(c + 1) * LANES] = cols.T.astype(BF16)


def _pack_w_in(w_in, layer):
    _, rows, cols = w_in.shape
    return pl.pallas_call(
        _pack_w_in_kernel,
        grid=(1,),
        in_specs=[pl.BlockSpec((1, cols, rows), lambda i: (layer, 0, 0))],
        out_specs=pl.BlockSpec((rows, C_END), lambda i: (0, 0)),
        out_shape=jax.ShapeDtypeStruct((rows, C_END), BF16),
        compiler_params=_params("arbitrary"),
        name="pack_w_in",
    )(jnp.swapaxes(w_in, 1, 2))


def _cast_kernel(w_ref, o_ref):
    o_ref[...] = w_ref[0].astype(o_ref.dtype)


def _layer_bf16(w, layer):
    _, rows, cols = w.shape
    block = min(rows, CAST_ROWS)
    return pl.pallas_call(
        _cast_kernel,
        grid=(rows // block,),
        in_specs=[pl.BlockSpec((1, block, cols), lambda i: (layer, i, 0))],
        out_specs=pl.BlockSpec((block, cols), lambda i: (i, 0)),
        out_shape=jax.ShapeDtypeStruct((rows, cols), BF16),
        compiler_params=_params("parallel"),
        name="cast_bf16",
    )(w)


def _block_diag(w):
    g, a, b = w.shape
    eye = jnp.eye(g, dtype=w.dtype)
    return (eye[:, None, :, None] * w[:, :, None, :]).reshape(g * a, g * b)


def kernel(x, positions, g_mix, w_in, w_gla_a, b_gla_a, g_gla_norm, w_pool, pool_scale, g_sgu, w_sgu, b_sgu,
           w_gate, w_branch, w_out, g_ffn, w_up, conv_w, conv_b, w_down, g_final):
    bsz, seq, d = x.shape
    assert bsz == 1 and d == D_MODEL and seq % FFN_TILE == 0 and seq % ROW_TILE == 0 and seq % GLA_TILE == 0
    depth = w_in.shape[0]
    xs = x[0]
    pos = positions.reshape(seq, 1)
    half = MOBA_DH // 2
    inv_freq = ROPE_THETA ** (-jnp.arange(half, dtype=F32) / half)
    invf = jnp.tile(inv_freq, LANES // half)[None, :]
    row = lambda a: a.reshape(1, -1)

    for l in range(depth):
        wa2 = jnp.concatenate([w_gla_a[l], jnp.zeros((LANES - GLA_RANK, GLA_QK), F32)], axis=0)
        gla_in, loc, qt3, k, vt3, kmean = _inproj(xs, pos, row(g_mix[l]), _pack_w_in(w_in, l), wa2,
                                                  row(b_gla_a[l]), invf)
        o_gla = _gla(gla_in, row(jnp.tile(g_gla_norm[l], GLA_HEADS)))
        o_moba = _moba(qt3, k.reshape(seq // MOBA_BLOCK, MOBA_BLOCK, MOBA_W), vt3,
                       kmean.reshape(seq // MOBA_BLOCK, MOBA_W))
        xs = _merge(xs, row(g_mix[l]), o_gla, o_moba, loc, _block_diag(w_pool[l]).astype(BF16),
                    row(pool_scale[l]), row(g_sgu[l]), w_sgu[l], jnp.repeat(b_sgu[l].T, SGU_GW, axis=1),
                    _layer_bf16(w_gate, l),
                    _layer_bf16(w_branch.reshape(depth, N_BRANCH * BRANCH_W, D_MODEL), l).reshape(
                        N_BRANCH, BRANCH_W, D_MODEL),
                    _layer_bf16(w_out, l))
        xs = _ffn(xs, row(g_ffn[l]), _layer_bf16(w_up, l), conv_w[l], row(conv_b[l]), _layer_bf16(w_down, l),
                  row(g_final), final_norm=(l == depth - 1))
    return xs[None]
```

```python
import functools

import numpy as np
import jax
import jax.numpy as jnp
from jax import lax
from jax.experimental import pallas as pl
from jax.experimental.pallas import tpu as pltpu

F32 = jnp.float32
BF16 = jnp.bfloat16
HIGHEST = lax.Precision.HIGHEST

D_MODEL = 1024
N_BRANCH = 4
BRANCH_W = 256
GLA_HEADS, GLA_DK, GLA_DV, GLA_RANK, GLA_TAU = 4, 32, 64, 16, 16.0
GLA_QK = GLA_HEADS * GLA_DK
GLA_V = GLA_HEADS * GLA_DV
POOL_WINDOWS = (2, 4, 8, 16)
POOL_GW = BRANCH_W // len(POOL_WINDOWS)
SGU_CHUNK, SGU_GROUPS = 128, 4
SGU_GW = BRANCH_W // SGU_GROUPS
MOBA_HEADS, MOBA_DH, MOBA_BLOCK, MOBA_TOPK = 4, 64, 256, 3
MOBA_W = MOBA_HEADS * MOBA_DH
MOBA_PV_ROWS = MOBA_DH + 16
MOBA_UNROLL = 4
MOBA_MAX_PARTS = 8
ROPE_THETA = 10000.0
D_FF = 2816
EPS = 1e-6

LANES = 128
VMEM_LIMIT_BYTES = 56 * 1024 * 1024

ROW_TILE = 512
FFN_TILE = 512
GLA_TILE = 256
GLA_SUB = 16
HALO = 16
FF_CHUNK = 256
CAST_ROWS = 256
NEG_BIG = -1e30
LOG2_E = float(np.log2(np.e))

C_GLA = 0
C_GA = C_GLA + 2 * GLA_QK + 2 * GLA_V
C_LOC = C_GA + LANES
C_MQ = C_LOC + BRANCH_W + 2 * BRANCH_W
C_MK = C_MQ + MOBA_W
C_MV = C_MK + MOBA_W
C_END = C_MV + MOBA_W
GLA_IN_W = C_GA + GLA_QK
LOC_W = 3 * BRANCH_W


def _rms(x, g):
    return x * lax.rsqrt(jnp.mean(x * x, axis=-1, keepdims=True) + EPS) * g


def _sigmoid(x):
    return 1.0 / (1.0 + jnp.exp(-x))


def _params(*sem):
    return pltpu.CompilerParams(dimension_semantics=sem, vmem_limit_bytes=VMEM_LIMIT_BYTES)


def _const_spec(shape):
    return pl.BlockSpec(shape, lambda i: (0,) * len(shape), pipeline_mode=pl.Buffered(1))


def _rope_tile(x, cos, sin, first_half):
    partner = jnp.where(first_half, -pltpu.roll(x, LANES - MOBA_DH // 2, axis=1),
                        pltpu.roll(x, MOBA_DH // 2, axis=1))
    return x * cos + partner * sin


def _inproj_kernel(x_ref, pos_ref, g_ref, w_ref, wa2_ref, ba_ref, invf_ref,
                   gla_ref, loc_ref, qt_ref, k_ref, vt_ref, kmean_ref):
    rows = x_ref.shape[0]
    h = _rms(x_ref[...], g_ref[...]).astype(BF16)
    proj = jnp.dot(h, w_ref[...], preferred_element_type=F32)
    proj_m = proj[:, C_MQ:C_END]
    loc_ref[...] = proj[:, C_LOC:C_MQ]

    ang = pos_ref[...].astype(F32) * invf_ref[...]
    cos, sin = jnp.cos(ang), jnp.sin(ang)
    lane = lax.broadcasted_iota(jnp.int32, (rows, LANES), 1)
    first_half = (lane % MOBA_DH) < (MOBA_DH // 2)

    def roped(c0):
        return jnp.concatenate(
            [_rope_tile(proj_m[:, c0 + t * LANES:c0 + (t + 1) * LANES], cos, sin, first_half)
             for t in range(MOBA_W // LANES)], axis=1)

    q = roped(0) * (MOBA_DH ** -0.5)
    k = roped(MOBA_W)
    v = proj_m[:, 2 * MOBA_W:]

    proj_g = proj[:, C_GLA:C_LOC]
    gla_ref[:, 0:C_GA] = proj_g[:, 0:C_GA]
    logit = jnp.dot(proj_g[:, C_GA:C_LOC], wa2_ref[...], precision=HIGHEST,
                    preferred_element_type=F32) + ba_ref[...]
    log_sig = jnp.minimum(logit, 0.0) - jnp.log(1.0 + jnp.exp(-jnp.abs(logit)))
    gla_ref[:, C_GA:GLA_IN_W] = log_sig * (1.0 / GLA_TAU)

    k_ref[...] = k.astype(BF16)
    for r in range(rows // MOBA_BLOCK):
        sl = slice(r * MOBA_BLOCK, (r + 1) * MOBA_BLOCK)
        qt_ref[r] = q[sl].T
        vt_ref[r] = v[sl].T.astype(BF16)
        kmean_ref[0, r:r + 1, :] = jnp.mean(k[sl], axis=0, keepdims=True)


def _inproj(x, pos, g, w_cat, wa2, ba, invf):
    seq = x.shape[0]
    nblk = ROW_TILE // MOBA_BLOCK
    row = lambda w: pl.BlockSpec((ROW_TILE, w), lambda i: (i, 0))
    blk3 = pl.BlockSpec((nblk, MOBA_W, MOBA_BLOCK), lambda i: (i, 0, 0))
    return pl.pallas_call(
        _inproj_kernel,
        grid=(seq // ROW_TILE,),
        in_specs=[row(D_MODEL), row(1), _const_spec((1, D_MODEL)), _const_spec((D_MODEL, C_END)),
                  _const_spec((LANES, GLA_QK)), _const_spec((1, GLA_QK)), _const_spec((1, LANES))],
        out_specs=[row(GLA_IN_W), row(LOC_W), blk3, row(MOBA_W), blk3,
                   pl.BlockSpec((1, nblk, MOBA_W), lambda i: (i, 0, 0))],
        out_shape=[jax.ShapeDtypeStruct((seq, GLA_IN_W), F32),
                   jax.ShapeDtypeStruct((seq, LOC_W), F32),
                   jax.ShapeDtypeStruct((seq // MOBA_BLOCK, MOBA_W, MOBA_BLOCK), F32),
                   jax.ShapeDtypeStruct((seq, MOBA_W), BF16),
                   jax.ShapeDtypeStruct((seq // MOBA_BLOCK, MOBA_W, MOBA_BLOCK), BF16),
                   jax.ShapeDtypeStruct((seq // ROW_TILE, nblk, MOBA_W), F32)],
        compiler_params=_params("parallel"),
        name="inproj",
    )(x, pos, g, w_cat, wa2, ba, invf)


def _gla_kernel(gin_ref, gnorm_ref, out_ref, state_ref, kpad_ref, bpad_ref,
                qb_ref, ke_ref, eb_ref, o_ref, kv_ref):
    rows = gin_ref.shape[0]

    @pl.when(pl.program_id(0) == 0)
    def _():
        state_ref[...] = jnp.zeros_like(state_ref)
        kpad_ref[0:HALO, :] = jnp.zeros((HALO, GLA_QK), F32)
        bpad_ref[0:HALO, :] = jnp.zeros((HALO, GLA_QK), F32)

    q = gin_ref[:, 0:GLA_QK] * (GLA_DK ** -0.5)
    k = gin_ref[:, GLA_QK:2 * GLA_QK]
    v = gin_ref[:, 2 * GLA_QK:2 * GLA_QK + GLA_V]
    la = gin_ref[:, C_GA:GLA_IN_W]

    rowmod = lax.broadcasted_iota(jnp.int32, (rows, GLA_QK), 0) % GLA_SUB
    b = la
    shift = 1
    while shift < GLA_SUB:
        b = b + jnp.where(rowmod >= shift, pltpu.roll(b, shift, axis=0), 0.0)
        shift *= 2
    b_last = jnp.broadcast_to(b.reshape(rows // GLA_SUB, GLA_SUB, GLA_QK)[:, GLA_SUB - 1:GLA_SUB, :],
                              (rows // GLA_SUB, GLA_SUB, GLA_QK)).reshape(rows, GLA_QK)
    eb = jnp.exp(b)
    qb_ref[...] = (q * eb).astype(BF16)
    ke_ref[...] = (k * jnp.exp(b_last - b)).astype(BF16)
    eb_ref[...] = eb

    kpad_ref[HALO:HALO + rows, :] = k
    bpad_ref[HALO:HALO + rows, :] = b

    lane = lax.broadcasted_iota(jnp.int32, (GLA_QK, GLA_QK), 1)
    head_sum = ((lax.broadcasted_iota(jnp.int32, (GLA_QK, GLA_QK), 0) // GLA_DK == lane // GLA_SUB) &
                (lane < GLA_HEADS * GLA_SUB)).astype(BF16)
    lane_j = lax.broadcasted_iota(jnp.int32, (rows, GLA_QK), 1) % GLA_SUB
    scores = jnp.zeros((rows, GLA_QK), F32)
    for d in range(GLA_SUB):
        lo = HALO - d
        kd = kpad_ref[lo:lo + rows, :]
        bd = bpad_ref[lo:lo + rows, :]
        decay = jnp.exp(jnp.where(rowmod >= d, b - bd, NEG_BIG))
        p = (q * kd * decay).astype(BF16)
        scores = jnp.where(lane_j == rowmod - d, jnp.dot(p, head_sum, preferred_element_type=F32), scores)
    scores = scores.astype(BF16)

    state_mask = (lax.broadcasted_iota(jnp.int32, (GLA_V, GLA_QK), 0) // GLA_DV ==
                  lax.broadcasted_iota(jnp.int32, (GLA_V, GLA_QK), 1) // GLA_DK)
    value_mask = (lax.broadcasted_iota(jnp.int32, (GLA_QK, GLA_V), 0) // GLA_SUB ==
                  lax.broadcasted_iota(jnp.int32, (GLA_QK, GLA_V), 1) // GLA_DV)

    steps = rows // GLA_SUB
    for s in range(steps):
        blk = slice(s * GLA_SUB, (s + 1) * GLA_SUB)
        v_b = v[blk].astype(BF16)
        v_heads = jnp.where(value_mask, jnp.concatenate([v_b] * (GLA_QK // GLA_SUB), axis=0), 0.0)
        o_ref[blk, :] = jnp.dot(scores[blk], v_heads, preferred_element_type=F32)
        kv = lax.dot_general(v_b, ke_ref[blk, :], (((0,), (0,)), ((), ())),
                             preferred_element_type=F32)
        kv_ref[s] = jnp.where(state_mask, kv, 0.0)

    state = state_ref[...]
    for s in range(steps):
        blk = slice(s * GLA_SUB, (s + 1) * GLA_SUB)
        o_ref[blk, :] += lax.dot_general(qb_ref[blk, :], state.astype(BF16),
                                         (((1,), (1,)), ((), ())), preferred_element_type=F32)
        state = state * eb_ref[(s + 1) * GLA_SUB - 1:(s + 1) * GLA_SUB, :] + kv_ref[s]
    state_ref[...] = state

    o = o_ref[...]
    head_mean = (lax.broadcasted_iota(jnp.int32, (GLA_V, GLA_V), 0) // GLA_DV ==
                 lax.broadcasted_iota(jnp.int32, (GLA_V, GLA_V), 1) // GLA_DV).astype(F32) * (1.0 / GLA_DV)
    ms = jnp.dot(o * o, head_mean, precision=HIGHEST, preferred_element_type=F32)
    r = gin_ref[:, 2 * GLA_QK + GLA_V:C_GA]
    out_ref[...] = o * lax.rsqrt(ms + EPS) * gnorm_ref[...] * (r * _sigmoid(r))


def _gla(gla_in, gnorm):
    seq = gla_in.shape[0]
    return pl.pallas_call(
        _gla_kernel,
        grid=(seq // GLA_TILE,),
        in_specs=[pl.BlockSpec((GLA_TILE, GLA_IN_W), lambda i: (i, 0)), _const_spec((1, GLA_V))],
        out_specs=pl.BlockSpec((GLA_TILE, GLA_V), lambda i: (i, 0)),
        out_shape=jax.ShapeDtypeStruct((seq, GLA_V), F32),
        scratch_shapes=[pltpu.VMEM((GLA_V, GLA_QK), F32),
                        pltpu.VMEM((HALO + GLA_TILE, GLA_QK), F32),
                        pltpu.VMEM((HALO + GLA_TILE, GLA_QK), F32),
                        pltpu.VMEM((GLA_TILE, GLA_QK), BF16),
                        pltpu.VMEM((GLA_TILE, GLA_QK), BF16),
                        pltpu.VMEM((GLA_TILE, GLA_QK), F32),
                        pltpu.VMEM((GLA_TILE, GLA_V), F32),
                        pltpu.VMEM((GLA_TILE // GLA_SUB, GLA_V, GLA_QK), F32)],
        compiler_params=_params("arbitrary"),
        name="gla",
    )(gla_in, gnorm)


def _local_mixers(loc_ref, halo_ref, wpool_ref, pscale_ref, gsgu_ref, wsgu_ref, bsgu_ref, zpad_ref):
    rows = loc_ref.shape[0]
    i = pl.program_id(0)

    z = loc_ref[:, 0:BRANCH_W]
    zpad_ref[0:HALO, :] = jnp.where(i > 0, halo_ref[...], 0.0)
    zpad_ref[HALO:HALO + rows, :] = z
    t = i * rows + lax.broadcasted_iota(jnp.int32, (rows, BRANCH_W), 0)
    group = lax.broadcasted_iota(jnp.int32, (rows, BRANCH_W), 1) // POOL_GW
    acc = zpad_ref[...]
    mean = jnp.zeros_like(z)
    width = 1
    for g, w in enumerate(POOL_WINDOWS):
        while width < w:
            acc = acc + pltpu.roll(acc, width, axis=0)
            width *= 2
        count = jnp.minimum(t + 1, w).astype(F32)
        mean = jnp.where(group == g, acc[HALO:] / count, mean)
    mixed = (mean - z).astype(BF16)
    pool = jnp.dot(mixed, wpool_ref[...], preferred_element_type=F32) * pscale_ref[...]

    zs = loc_ref[:, BRANCH_W:LOC_W]
    gz = 0.5 * zs * (1.0 + jnp.tanh(np.sqrt(2.0 / np.pi).astype(np.float32) * (zs + 0.044715 * (zs * zs * zs))))
    u = gz[:, 0:BRANCH_W]
    vn = _rms(gz[:, BRANCH_W:], gsgu_ref[...]).astype(BF16)
    tril = (lax.broadcasted_iota(jnp.int32, (SGU_CHUNK, SGU_CHUNK), 0) >=
            lax.broadcasted_iota(jnp.int32, (SGU_CHUNK, SGU_CHUNK), 1))
    w_all = jnp.concatenate([jnp.where(tril, wsgu_ref[g], 0.0).astype(BF16) for g in range(SGU_GROUPS)], axis=1)
    cgroup = lax.broadcasted_iota(jnp.int32, (SGU_CHUNK, BRANCH_W), 1) // SGU_GW
    sgu = []
    for c in range(rows // SGU_CHUNK):
        sl = slice(c * SGU_CHUNK, (c + 1) * SGU_CHUNK)
        v_grp = jnp.concatenate([jnp.where(cgroup == g, vn[sl], 0.0) for g in range(SGU_GROUPS)], axis=0)
        sv = jnp.dot(w_all, v_grp, preferred_element_type=F32) + bsgu_ref[...]
        sgu.append(u[sl] * sv)
    return pool, jnp.concatenate(sgu, axis=0)


def _moba_kernel(qt_ref, k_ref, vt_ref, kmean_ref, o_ref, sel_ref, qtb_ref, m_ref, acc_ref, out_ref, s_ref):
    own = pl.program_id(0)
    nblk = k_ref.shape[0]
    blk_id = lax.broadcasted_iota(jnp.int32, (nblk, MOBA_BLOCK), 0)
    pair_row = lax.broadcasted_iota(jnp.int32, (LANES, MOBA_BLOCK), 0) // MOBA_DH
    key_pos = lax.broadcasted_iota(jnp.int32, (MOBA_BLOCK, MOBA_BLOCK), 0)
    q_pos = lax.broadcasted_iota(jnp.int32, (MOBA_BLOCK, MOBA_BLOCK), 1)
    ones_rows = (lax.broadcasted_iota(jnp.int32, (MOBA_PV_ROWS - MOBA_DH, MOBA_BLOCK), 0) == 0).astype(BF16)

    for h in range(MOBA_HEADS):
        lanes = pl.ds((h // 2) * LANES, LANES)
        qt = jnp.where(pair_row == (h % 2), qt_ref[0, (h // 2) * LANES:(h // 2 + 1) * LANES, :], 0.0)
        qtb_ref[h] = (qt * LOG2_E).astype(BF16)

        gate = jnp.dot(kmean_ref[:, lanes], qt, precision=HIGHEST, preferred_element_type=F32)
        gate = jnp.where(blk_id < own, gate, -jnp.inf)
        sel = jnp.zeros((nblk, MOBA_BLOCK), F32)
        for _ in range(MOBA_TOPK):
            best = jnp.max(gate, axis=0, keepdims=True)
            first = jnp.min(jnp.where(gate == best, blk_id, nblk), axis=0, keepdims=True)
            pick = (blk_id == first) & (best > -jnp.inf)
            sel = jnp.where(pick, 1.0, sel)
            gate = jnp.where(pick, -jnp.inf, gate)
        sel_ref[h] = sel
        m_ref[h] = jnp.full((1, MOBA_BLOCK), NEG_BIG, F32)
        acc_ref[h] = jnp.zeros((MOBA_PV_ROWS, MOBA_BLOCK), F32)

    def scores(j, h, slot):
        lanes = pl.ds((h // 2) * LANES, LANES)
        s_ref[slot * MOBA_HEADS + h] = jnp.dot(k_ref[j, :, lanes], qtb_ref[h],
                                               preferred_element_type=F32).astype(BF16)

    def attend(j, h, slot, selected, causal):
        s = s_ref[slot * MOBA_HEADS + h]
        if causal:
            s = jnp.where(key_pos <= q_pos, s, NEG_BIG)
        m_run = m_ref[h]
        parts = s.reshape(MOBA_MAX_PARTS, MOBA_BLOCK // MOBA_MAX_PARTS, MOBA_BLOCK)
        blk_max = jnp.max(jnp.max(parts, axis=0), axis=0, keepdims=True)
        m_new = jnp.maximum(m_run, jnp.where(selected, blk_max.astype(F32), NEG_BIG))
        shift = jnp.where(selected, m_new, -NEG_BIG).astype(BF16)
        p = jnp.exp2(s - shift)
        v_aug = jnp.concatenate([vt_ref[j, pl.ds(h * MOBA_DH, MOBA_DH), :], ones_rows], axis=0)
        pv = jnp.dot(v_aug, p, preferred_element_type=F32)
        acc_ref[h] = acc_ref[h] * jnp.exp2(m_run - m_new) + pv
        m_ref[h] = m_new

    def past(j, slot):
        for pair in range(MOBA_HEADS // 2):
            scores(j + 1, 2 * pair, 1 - slot)
            scores(j + 1, 2 * pair + 1, 1 - slot)
            for h in (2 * pair, 2 * pair + 1):
                attend(j, h, slot, sel_ref[h, pl.ds(j, 1), :] > 0.0, False)

    def past_group(t, carry):
        for u in range(MOBA_UNROLL):
            past(MOBA_UNROLL * t + u, u % 2)
        return carry

    def own_block(slot):
        everyone = jnp.full((1, MOBA_BLOCK), True)
        for h in range(MOBA_HEADS):
            attend(own, h, slot, everyone, True)

    for h in range(MOBA_HEADS):
        scores(0, h, 0)
    lax.fori_loop(0, own // MOBA_UNROLL, past_group, 0)
    done = (own // MOBA_UNROLL) * MOBA_UNROLL
    for u in range(MOBA_UNROLL - 1):

        @pl.when(own - done > u)
        def _():
            past(done + u, u % 2)

    @pl.when(own % 2 == 0)
    def _():
        own_block(0)

    @pl.when(own % 2 == 1)
    def _():
        own_block(1)

    for h in range(MOBA_HEADS):
        out_ref[h * MOBA_DH:(h + 1) * MOBA_DH, :] = acc_ref[h, 0:MOBA_DH, :] / acc_ref[h, MOBA_DH:MOBA_DH + 1, :]
    o_ref[...] = out_ref[...].T


def _moba(qt3, k3, vt3, kmean):
    nblk = k3.shape[0]
    seq = nblk * MOBA_BLOCK
    return pl.pallas_call(
        _moba_kernel,
        grid=(nblk,),
        in_specs=[pl.BlockSpec((1, MOBA_W, MOBA_BLOCK), lambda i: (i, 0, 0)),
                  _const_spec((nblk, MOBA_BLOCK, MOBA_W)), _const_spec((nblk, MOBA_W, MOBA_BLOCK)),
                  _const_spec((nblk, MOBA_W))],
        out_specs=pl.BlockSpec((MOBA_BLOCK, MOBA_W), lambda i: (i, 0)),
        out_shape=jax.ShapeDtypeStruct((seq, MOBA_W), F32),
        scratch_shapes=[pltpu.VMEM((MOBA_HEADS, nblk, MOBA_BLOCK), F32),
                        pltpu.VMEM((MOBA_HEADS, LANES, MOBA_BLOCK), BF16),
                        pltpu.VMEM((MOBA_HEADS, 1, MOBA_BLOCK), F32),
                        pltpu.VMEM((MOBA_HEADS, MOBA_PV_ROWS, MOBA_BLOCK), F32),
                        pltpu.VMEM((MOBA_W, MOBA_BLOCK), F32),
                        pltpu.VMEM((2 * MOBA_HEADS, MOBA_BLOCK, MOBA_BLOCK), BF16)],
        compiler_params=_params("parallel"),
        name="moba",
    )(qt3, k3, vt3, kmean)


def _merge_kernel(x_ref, g_ref, gla_ref, moba_ref, loc_ref, halo_ref, wpool_ref, pscale_ref, gsgu_ref, wsgu_ref,
                  bsgu_ref, wgate_ref, wbr_ref, wout_ref, y_ref, zpad_ref):
    x = x_ref[...]
    h = _rms(x, g_ref[...]).astype(BF16)

    def gated(b, branch):
        gate = _sigmoid(jnp.dot(h, wgate_ref[:, b * D_MODEL:(b + 1) * D_MODEL], preferred_element_type=F32))
        return gate * jnp.dot(branch.astype(BF16), wbr_ref[b], preferred_element_type=F32)

    merged = gated(0, gla_ref[...]) + gated(3, moba_ref[...])
    pool, sgu = _local_mixers(loc_ref, halo_ref, wpool_ref, pscale_ref, gsgu_ref, wsgu_ref, bsgu_ref, zpad_ref)
    merged = merged + gated(1, pool) + gated(2, sgu)
    y_ref[...] = x + jnp.dot(merged.astype(BF16), wout_ref[...], preferred_element_type=F32)


def _merge(x, g, o_gla, o_moba, loc, wpool_bd, pscale, gsgu, wsgu, bsgu_full, wgate, wbr, wout):
    seq = x.shape[0]
    per = ROW_TILE // HALO
    row = lambda w: pl.BlockSpec((ROW_TILE, w), lambda i: (i, 0))
    return pl.pallas_call(
        _merge_kernel,
        grid=(seq // ROW_TILE,),
        in_specs=[row(D_MODEL), _const_spec((1, D_MODEL)), row(BRANCH_W), row(BRANCH_W), row(LOC_W),
                  pl.BlockSpec((HALO, BRANCH_W), lambda i: (jnp.maximum(i * per - 1, 0), 0)),
                  _const_spec((BRANCH_W, BRANCH_W)), _const_spec((1, BRANCH_W)), _const_spec((1, BRANCH_W)),
                  _const_spec((SGU_GROUPS, SGU_CHUNK, SGU_CHUNK)), _const_spec((SGU_CHUNK, BRANCH_W)),
                  _const_spec((D_MODEL, N_BRANCH * D_MODEL)), _const_spec((N_BRANCH, BRANCH_W, D_MODEL)),
                  _const_spec((D_MODEL, D_MODEL))],
        out_specs=row(D_MODEL),
        out_shape=jax.ShapeDtypeStruct((seq, D_MODEL), F32),
        scratch_shapes=[pltpu.VMEM((HALO + ROW_TILE, BRANCH_W), F32)],
        compiler_params=_params("parallel"),
        name="merge",
    )(x, g, o_gla, o_moba, loc, loc, wpool_bd, pscale, gsgu, wsgu, bsgu_full, wgate, wbr, wout)


def _ffn_kernel(x_ref, halo_ref, g_ref, wup_ref, cw_ref, cb_ref, wdown_ref, gfin_ref, y_ref,
                h_ref, acc_ref, up_ref, act_ref, *, final_norm):
    rows = x_ref.shape[0]
    x = x_ref[...]
    h_halo = _rms(halo_ref[...], g_ref[...])
    h_ref[0:HALO, :] = jnp.where(pl.program_id(0) > 0, h_halo, 0.0).astype(BF16)
    h_ref[HALO:HALO + rows, :] = _rms(x, g_ref[...]).astype(BF16)
    n_chunks = D_FF // FF_CHUNK

    def up_proj(c, slot):
        for half in range(2):
            cols = pl.ds(half * D_FF + c * FF_CHUNK, FF_CHUNK)
            up_ref[slot, half] = jnp.dot(h_ref[...], wup_ref[:, cols], preferred_element_type=F32)

    def conv(c, slot, half):
        cols = pl.ds(half * D_FF + c * FF_CHUNK, FF_CHUNK)
        cw = cw_ref[:, cols]
        return (up_ref[slot, half, HALO - 2:HALO - 2 + rows, :] * cw[0:1] +
                up_ref[slot, half, HALO - 1:HALO - 1 + rows, :] * cw[1:2] +
                up_ref[slot, half, HALO:HALO + rows, :] * cw[2:3] + cb_ref[:, cols])

    acc_ref[...] = x
    up_proj(0, 0)
    for c in range(n_chunks + 1):
        if c + 1 < n_chunks:
            up_proj(c + 1, (c + 1) % 2)
        if c >= 1:
            acc_ref[...] += jnp.dot(act_ref[(c - 1) % 2], wdown_ref[(c - 1) * FF_CHUNK:c * FF_CHUNK, :],
                                    preferred_element_type=F32)
        if c < n_chunks:
            a = conv(c, c % 2, 0)
            b = conv(c, c % 2, 1)
            act_ref[c % 2] = (a * _sigmoid(a) * b).astype(BF16)
    y = acc_ref[...]
    y_ref[...] = _rms(y, gfin_ref[...]) if final_norm else y


def _ffn(x, g, wup, cw, cb, wdown, gfin, final_norm):
    seq = x.shape[0]
    per = FFN_TILE // HALO
    row = pl.BlockSpec((FFN_TILE, D_MODEL), lambda i: (i, 0))
    return pl.pallas_call(
        functools.partial(_ffn_kernel, final_norm=final_norm),
        grid=(seq // FFN_TILE,),
        in_specs=[row, pl.BlockSpec((HALO, D_MODEL), lambda i: (jnp.maximum(i * per - 1, 0), 0)),
                  _const_spec((1, D_MODEL)), _const_spec((D_MODEL, 2 * D_FF)), _const_spec((3, 2 * D_FF)),
                  _const_spec((1, 2 * D_FF)), _const_spec((D_FF, D_MODEL)), _const_spec((1, D_MODEL))],
        out_specs=row,
        out_shape=jax.ShapeDtypeStruct((seq, D_MODEL), F32),
        scratch_shapes=[pltpu.VMEM((HALO + FFN_TILE, D_MODEL), BF16), pltpu.VMEM((FFN_TILE, D_MODEL), F32),
                        pltpu.VMEM((2, 2, HALO + FFN_TILE, FF_CHUNK), F32),
                        pltpu.VMEM((2, FFN_TILE, FF_CHUNK), BF16)],
        compiler_params=_params("parallel"),
        name="conv_ffn",
    )(x, x, g, wup, cw, cb, wdown, gfin)


def _pack_w_in_kernel(wt_ref, o_ref):
    n_gate_end = C_GA + GLA_RANK
    d_model = wt_ref.shape[2]
    for c in range(C_END // LANES):
        if (c + 1) * LANES <= C_GA:
            cols = wt_ref[0, c * LANES:(c + 1) * LANES, :]
        elif c * LANES == C_GA:
            cols = jnp.concatenate([wt_ref[0, C_GA:n_gate_end, :],
                                    jnp.zeros((LANES - GLA_RANK, d_model), F32)], axis=0)
        else:
            src = n_gate_end + c * LANES - C_LOC
            cols = wt_ref[0, src:src + LANES, :]
        o_ref[:, c * LANES:(c + 1) * LANES] = cols.T.astype(BF16)


def _pack_w_in(w_in, layer):
    _, rows, cols = w_in.shape
    return pl.pallas_call(
        _pack_w_in_kernel,
        grid=(1,),
        in_specs=[pl.BlockSpec((1, cols, rows), lambda i: (layer, 0, 0))],
        out_specs=pl.BlockSpec((rows, C_END), lambda i: (0, 0)),
        out_shape=jax.ShapeDtypeStruct((rows, C_END), BF16),
        compiler_params=_params("arbitrary"),
        name="pack_w_in",
    )(jnp.swapaxes(w_in, 1, 2))


def _cast_kernel(w_ref, o_ref):
    o_ref[...] = w_ref[0].astype(o_ref.dtype)


def _layer_bf16(w, layer):
    _, rows, cols = w.shape
    block = min(rows, CAST_ROWS)
    return pl.pallas_call(
        _cast_kernel,
        grid=(rows // block,),
        in_specs=[pl.BlockSpec((1, block, cols), lambda i: (layer, i, 0))],
        out_specs=pl.BlockSpec((block, cols), lambda i: (i, 0)),
        out_shape=jax.ShapeDtypeStruct((rows, cols), BF16),
        compiler_params=_params("parallel"),
        name="cast_bf16",
    )(w)


def _block_diag(w):
    g, a, b = w.shape
    eye = jnp.eye(g, dtype=w.dtype)
    return (eye[:, None, :, None] * w[:, :, None, :]).reshape(g * a, g * b)


def kernel(x, positions, g_mix, w_in, w_gla_a, b_gla_a, g_gla_norm, w_pool, pool_scale, g_sgu, w_sgu, b_sgu,
           w_gate, w_branch, w_out, g_ffn, w_up, conv_w, conv_b, w_down, g_final):
    bsz, seq, d = x.shape
    assert bsz == 1 and d == D_MODEL and seq % FFN_TILE == 0 and seq % ROW_TILE == 0 and seq % GLA_TILE == 0
    depth = w_in.shape[0]
    xs = x[0]
    pos = positions.reshape(seq, 1)
    half = MOBA_DH // 2
    inv_freq = ROPE_THETA ** (-jnp.arange(half, dtype=F32) / half)
    invf = jnp.tile(inv_freq, LANES // half)[None, :]
    row = lambda a: a.reshape(1, -1)

    for l in range(depth):
        wa2 = jnp.concatenate([w_gla_a[l], jnp.zeros((LANES - GLA_RANK, GLA_QK), F32)], axis=0)
        gla_in, loc, qt3, k, vt3, kmean = _inproj(xs, pos, row(g_mix[l]), _pack_w_in(w_in, l), wa2,
                                                  row(b_gla_a[l]), invf)
        o_gla = _gla(gla_in, row(jnp.tile(g_gla_norm[l], GLA_HEADS)))
        o_moba = _moba(qt3, k.reshape(seq // MOBA_BLOCK, MOBA_BLOCK, MOBA_W), vt3,
                       kmean.reshape(seq // MOBA_BLOCK, MOBA_W))
        xs = _merge(xs, row(g_mix[l]), o_gla, o_moba, loc, _block_diag(w_pool[l]).astype(BF16),
                    row(pool_scale[l]), row(g_sgu[l]), w_sgu[l], jnp.repeat(b_sgu[l].T, SGU_GW, axis=1),
                    _layer_bf16(w_gate, l),
                    _layer_bf16(w_branch.reshape(depth, N_BRANCH * BRANCH_W, D_MODEL), l).reshape(
                        N_BRANCH, BRANCH_W, D_MODEL),
                    _layer_bf16(w_out, l))
        xs = _ffn(xs, row(g_ffn[l]), _layer_bf16(w_up, l), conv_w[l], row(conv_b[l]), _layer_bf16(w_down, l),
                  row(g_final), final_norm=(l == depth - 1))
    return xs[None]
```

```python
import functools

import numpy as np
import jax
import jax.numpy as jnp
from jax import lax
from jax.experimental import pallas as pl
from jax.experimental.pallas import tpu as pltpu

F32 = jnp.float32
BF16 = jnp.bfloat16
HIGHEST = lax.Precision.HIGHEST

D_MODEL = 1024
N_BRANCH = 4
BRANCH_W = 256
GLA_HEADS, GLA_DK, GLA_DV, GLA_RANK, GLA_TAU = 4, 32, 64, 16, 16.0
GLA_QK = GLA_HEADS * GLA_DK
GLA_V = GLA_HEADS * GLA_DV
POOL_WINDOWS = (2, 4, 8, 16)
POOL_GW = BRANCH_W // len(POOL_WINDOWS)
SGU_CHUNK, SGU_GROUPS = 128, 4
SGU_GW = BRANCH_W // SGU_GROUPS
MOBA_HEADS, MOBA_DH, MOBA_BLOCK, MOBA_TOPK = 4, 64, 256, 3
MOBA_W = MOBA_HEADS * MOBA_DH
MOBA_PV_ROWS = MOBA_DH + 16
MOBA_UNROLL = 4
MOBA_MAX_PARTS = 8
ROPE_THETA = 10000.0
D_FF = 2816
EPS = 1e-6

LANES = 128
VMEM_LIMIT_BYTES = 56 * 1024 * 1024

ROW_TILE = 512
FFN_TILE = 512
GLA_TILE = 512
GLA_SUB = 16
HALO = 16
FF_CHUNK = 256
CAST_ROWS = 256
NEG_BIG = -1e30
LOG2_E = float(np.log2(np.e))

C_GLA = 0
C_GA = C_GLA + 2 * GLA_QK + 2 * GLA_V
C_LOC = C_GA + LANES
C_MQ = C_LOC + BRANCH_W + 2 * BRANCH_W
C_MK = C_MQ + MOBA_W
C_MV = C_MK + MOBA_W
C_END = C_MV + MOBA_W
GLA_IN_W = C_GA + GLA_QK
LOC_W = 3 * BRANCH_W


def _rms(x, g):
    return x * lax.rsqrt(jnp.mean(x * x, axis=-1, keepdims=True) + EPS) * g


def _sigmoid(x):
    return 1.0 / (1.0 + jnp.exp(-x))


def _params(*sem):
    return pltpu.CompilerParams(dimension_semantics=sem, vmem_limit_bytes=VMEM_LIMIT_BYTES)


def _const_spec(shape):
    return pl.BlockSpec(shape, lambda i: (0,) * len(shape), pipeline_mode=pl.Buffered(1))


def _rope_tile(x, cos, sin, first_half):
    partner = jnp.where(first_half, -pltpu.roll(x, LANES - MOBA_DH // 2, axis=1),
                        pltpu.roll(x, MOBA_DH // 2, axis=1))
    return x * cos + partner * sin


def _inproj_kernel(x_ref, pos_ref, g_ref, w_ref, wa2_ref, ba_ref, invf_ref,
                   gla_ref, loc_ref, qt_ref, k_ref, vt_ref, kmean_ref):
    rows = x_ref.shape[0]
    h = _rms(x_ref[...], g_ref[...]).astype(BF16)
    proj = jnp.dot(h, w_ref[...], preferred_element_type=F32)
    proj_m = proj[:, C_MQ:C_END]
    loc_ref[...] = proj[:, C_LOC:C_MQ]

    ang = pos_ref[...].astype(F32) * invf_ref[...]
    cos, sin = jnp.cos(ang), jnp.sin(ang)
    lane = lax.broadcasted_iota(jnp.int32, (rows, LANES), 1)
    first_half = (lane % MOBA_DH) < (MOBA_DH // 2)

    def roped(c0):
        return jnp.concatenate(
            [_rope_tile(proj_m[:, c0 + t * LANES:c0 + (t + 1) * LANES], cos, sin, first_half)
             for t in range(MOBA_W // LANES)], axis=1)

    q = roped(0) * (MOBA_DH ** -0.5)
    k = roped(MOBA_W)
    v = proj_m[:, 2 * MOBA_W:]

    proj_g = proj[:, C_GLA:C_LOC]
    gla_ref[:, 0:C_GA] = proj_g[:, 0:C_GA]
    logit = jnp.dot(proj_g[:, C_GA:C_LOC], wa2_ref[...], precision=HIGHEST,
                    preferred_element_type=F32) + ba_ref[...]
    log_sig = jnp.minimum(logit, 0.0) - jnp.log(1.0 + jnp.exp(-jnp.abs(logit)))
    gla_ref[:, C_GA:GLA_IN_W] = log_sig * (1.0 / GLA_TAU)

    k_ref[...] = k.astype(BF16)
    for r in range(rows // MOBA_BLOCK):
        sl = slice(r * MOBA_BLOCK, (r + 1) * MOBA_BLOCK)
        qt_ref[r] = q[sl].T
        vt_ref[r] = v[sl].T.astype(BF16)
        kmean_ref[0, r:r + 1, :] = jnp.mean(k[sl], axis=0, keepdims=True)


def _inproj(x, pos, g, w_cat, wa2, ba, invf):
    seq = x.shape[0]
    nblk = ROW_TILE // MOBA_BLOCK
    row = lambda w: pl.BlockSpec((ROW_TILE, w), lambda i: (i, 0))
    blk3 = pl.BlockSpec((nblk, MOBA_W, MOBA_BLOCK), lambda i: (i, 0, 0))
    return pl.pallas_call(
        _inproj_kernel,
        grid=(seq // ROW_TILE,),
        in_specs=[row(D_MODEL), row(1), _const_spec((1, D_MODEL)), _const_spec((D_MODEL, C_END)),
                  _const_spec((LANES, GLA_QK)), _const_spec((1, GLA_QK)), _const_spec((1, LANES))],
        out_specs=[row(GLA_IN_W), row(LOC_W), blk3, row(MOBA_W), blk3,
                   pl.BlockSpec((1, nblk, MOBA_W), lambda i: (i, 0, 0))],
        out_shape=[jax.ShapeDtypeStruct((seq, GLA_IN_W), F32),
                   jax.ShapeDtypeStruct((seq, LOC_W), F32),
                   jax.ShapeDtypeStruct((seq // MOBA_BLOCK, MOBA_W, MOBA_BLOCK), F32),
                   jax.ShapeDtypeStruct((seq, MOBA_W), BF16),
                   jax.ShapeDtypeStruct((seq // MOBA_BLOCK, MOBA_W, MOBA_BLOCK), BF16),
                   jax.ShapeDtypeStruct((seq // ROW_TILE, nblk, MOBA_W), F32)],
        compiler_params=_params("parallel"),
        name="inproj",
    )(x, pos, g, w_cat, wa2, ba, invf)


def _gla_kernel(gin_ref, gnorm_ref, out_ref, state_ref, kpad_ref, bpad_ref,
                qb_ref, ke_ref, eb_ref, o_ref, kv_ref):
    rows = gin_ref.shape[0]

    @pl.when(pl.program_id(0) == 0)
    def _():
        state_ref[...] = jnp.zeros_like(state_ref)
        kpad_ref[0:HALO, :] = jnp.zeros((HALO, GLA_QK), F32)
        bpad_ref[0:HALO, :] = jnp.zeros((HALO, GLA_QK), F32)

    q = gin_ref[:, 0:GLA_QK] * (GLA_DK ** -0.5)
    k = gin_ref[:, GLA_QK:2 * GLA_QK]
    v = gin_ref[:, 2 * GLA_QK:2 * GLA_QK + GLA_V]
    la = gin_ref[:, C_GA:GLA_IN_W]

    rowmod = lax.broadcasted_iota(jnp.int32, (rows, GLA_QK), 0) % GLA_SUB
    b = la
    shift = 1
    while shift < GLA_SUB:
        b = b + jnp.where(rowmod >= shift, pltpu.roll(b, shift, axis=0), 0.0)
        shift *= 2
    b_last = jnp.broadcast_to(b.reshape(rows // GLA_SUB, GLA_SUB, GLA_QK)[:, GLA_SUB - 1:GLA_SUB, :],
                              (rows // GLA_SUB, GLA_SUB, GLA_QK)).reshape(rows, GLA_QK)
    eb = jnp.exp(b)
    qb_ref[...] = (q * eb).astype(BF16)
    ke_ref[...] = (k * jnp.exp(b_last - b)).astype(BF16)
    eb_ref[...] = eb

    kpad_ref[HALO:HALO + rows, :] = k
    bpad_ref[HALO:HALO + rows, :] = b

    lane = lax.broadcasted_iota(jnp.int32, (GLA_QK, GLA_QK), 1)
    head_sum = ((lax.broadcasted_iota(jnp.int32, (GLA_QK, GLA_QK), 0) // GLA_DK == lane // GLA_SUB) &
                (lane < GLA_HEADS * GLA_SUB)).astype(BF16)
    lane_j = lax.broadcasted_iota(jnp.int32, (rows, GLA_QK), 1) % GLA_SUB
    scores = jnp.zeros((rows, GLA_QK), F32)
    for d in range(GLA_SUB):
        lo = HALO - d
        kd = kpad_ref[lo:lo + rows, :]
        bd = bpad_ref[lo:lo + rows, :]
        decay = jnp.exp(jnp.where(rowmod >= d, b - bd, NEG_BIG))
        p = (q * kd * decay).astype(BF16)
        scores = jnp.where(lane_j == rowmod - d, jnp.dot(p, head_sum, preferred_element_type=F32), scores)
    scores = scores.astype(BF16)

    state_mask = (lax.broadcasted_iota(jnp.int32, (GLA_V, GLA_QK), 0) // GLA_DV ==
                  lax.broadcasted_iota(jnp.int32, (GLA_V, GLA_QK), 1) // GLA_DK)
    value_mask = (lax.broadcasted_iota(jnp.int32, (GLA_QK, GLA_V), 0) // GLA_SUB ==
                  lax.broadcasted_iota(jnp.int32, (GLA_QK, GLA_V), 1) // GLA_DV)

    steps = rows // GLA_SUB
    for s in range(steps):
        blk = slice(s * GLA_SUB, (s + 1) * GLA_SUB)
        v_b = v[blk].astype(BF16)
        v_heads = jnp.where(value_mask, jnp.concatenate([v_b] * (GLA_QK // GLA_SUB), axis=0), 0.0)
        o_ref[blk, :] = jnp.dot(scores[blk], v_heads, preferred_element_type=F32)
        kv = lax.dot_general(v_b, ke_ref[blk, :], (((0,), (0,)), ((), ())),
                             preferred_element_type=F32)
        kv_ref[s] = jnp.where(state_mask, kv, 0.0)

    state = state_ref[...]
    for s in range(steps):
        blk = slice(s * GLA_SUB, (s + 1) * GLA_SUB)
        o_ref[blk, :] += lax.dot_general(qb_ref[blk, :], state.astype(BF16),
                                         (((1,), (1,)), ((), ())), preferred_element_type=F32)
        state = state * eb_ref[(s + 1) * GLA_SUB - 1:(s + 1) * GLA_SUB, :] + kv_ref[s]
    state_ref[...] = state

    o = o_ref[...]
    head_mean = (lax.broadcasted_iota(jnp.int32, (GLA_V, GLA_V), 0) // GLA_DV ==
                 lax.broadcasted_iota(jnp.int32, (GLA_V, GLA_V), 1) // GLA_DV).astype(F32) * (1.0 / GLA_DV)
    ms = jnp.dot(o * o, head_mean, precision=HIGHEST, preferred_element_type=F32)
    r = gin_ref[:, 2 * GLA_QK + GLA_V:C_GA]
    out_ref[...] = o * lax.rsqrt(ms + EPS) * gnorm_ref[...] * (r * _sigmoid(r))


def _gla(gla_in, gnorm):
    seq = gla_in.shape[0]
    return pl.pallas_call(
        _gla_kernel,
        grid=(seq // GLA_TILE,),
        in_specs=[pl.BlockSpec((GLA_TILE, GLA_IN_W), lambda i: (i, 0)), _const_spec((1, GLA_V))],
        out_specs=pl.BlockSpec((GLA_TILE, GLA_V), lambda i: (i, 0)),
        out_shape=jax.ShapeDtypeStruct((seq, GLA_V), F32),
        scratch_shapes=[pltpu.VMEM((GLA_V, GLA_QK), F32),
                        pltpu.VMEM((HALO + GLA_TILE, GLA_QK), F32),
                        pltpu.VMEM((HALO + GLA_TILE, GLA_QK), F32),
                        pltpu.VMEM((GLA_TILE, GLA_QK), BF16),
                        pltpu.VMEM((GLA_TILE, GLA_QK), BF16),
                        pltpu.VMEM((GLA_TILE, GLA_QK), F32),
                        pltpu.VMEM((GLA_TILE, GLA_V), F32),
                        pltpu.VMEM((GLA_TILE // GLA_SUB, GLA_V, GLA_QK), F32)],
        compiler_params=_params("arbitrary"),
        name="gla",
    )(gla_in, gnorm)


def _local_mixers(loc_ref, halo_ref, wpool_ref, pscale_ref, gsgu_ref, wsgu_ref, bsgu_ref, zpad_ref):
    rows = loc_ref.shape[0]
    i = pl.program_id(0)

    z = loc_ref[:, 0:BRANCH_W]
    zpad_ref[0:HALO, :] = jnp.where(i > 0, halo_ref[...], 0.0)
    zpad_ref[HALO:HALO + rows, :] = z
    t = i * rows + lax.broadcasted_iota(jnp.int32, (rows, BRANCH_W), 0)
    group = lax.broadcasted_iota(jnp.int32, (rows, BRANCH_W), 1) // POOL_GW
    acc = zpad_ref[...]
    mean = jnp.zeros_like(z)
    width = 1
    for g, w in enumerate(POOL_WINDOWS):
        while width < w:
            acc = acc + pltpu.roll(acc, width, axis=0)
            width *= 2
        count = jnp.minimum(t + 1, w).astype(F32)
        mean = jnp.where(group == g, acc[HALO:] / count, mean)
    mixed = (mean - z).astype(BF16)
    pool = jnp.dot(mixed, wpool_ref[...], preferred_element_type=F32) * pscale_ref[...]

    zs = loc_ref[:, BRANCH_W:LOC_W]
    gz = 0.5 * zs * (1.0 + jnp.tanh(np.sqrt(2.0 / np.pi).astype(np.float32) * (zs + 0.044715 * (zs * zs * zs))))
    u = gz[:, 0:BRANCH_W]
    vn = _rms(gz[:, BRANCH_W:], gsgu_ref[...]).astype(BF16)
    tril = (lax.broadcasted_iota(jnp.int32, (SGU_CHUNK, SGU_CHUNK), 0) >=
            lax.broadcasted_iota(jnp.int32, (SGU_CHUNK, SGU_CHUNK), 1))
    w_all = jnp.concatenate([jnp.where(tril, wsgu_ref[g], 0.0).astype(BF16) for g in range(SGU_GROUPS)], axis=1)
    cgroup = lax.broadcasted_iota(jnp.int32, (SGU_CHUNK, BRANCH_W), 1) // SGU_GW
    sgu = []
    for c in range(rows // SGU_CHUNK):
        sl = slice(c * SGU_CHUNK, (c + 1) * SGU_CHUNK)
        v_grp = jnp.concatenate([jnp.where(cgroup == g, vn[sl], 0.0) for g in range(SGU_GROUPS)], axis=0)
        sv = jnp.dot(w_all, v_grp, preferred_element_type=F32) + bsgu_ref[...]
        sgu.append(u[sl] * sv)
    return pool, jnp.concatenate(sgu, axis=0)


def _moba_kernel(qt_ref, k_ref, vt_ref, kmean_ref, o_ref, sel_ref, qtb_ref, m_ref, acc_ref, out_ref, s_ref):
    own = pl.program_id(0)
    nblk = k_ref.shape[0]
    blk_id = lax.broadcasted_iota(jnp.int32, (nblk, MOBA_BLOCK), 0)
    pair_row = lax.broadcasted_iota(jnp.int32, (LANES, MOBA_BLOCK), 0) // MOBA_DH
    key_pos = lax.broadcasted_iota(jnp.int32, (MOBA_BLOCK, MOBA_BLOCK), 0)
    q_pos = lax.broadcasted_iota(jnp.int32, (MOBA_BLOCK, MOBA_BLOCK), 1)
    ones_rows = (lax.broadcasted_iota(jnp.int32, (MOBA_PV_ROWS - MOBA_DH, MOBA_BLOCK), 0) == 0).astype(BF16)

    for h in range(MOBA_HEADS):
        lanes = pl.ds((h // 2) * LANES, LANES)
        qt = jnp.where(pair_row == (h % 2), qt_ref[0, (h // 2) * LANES:(h // 2 + 1) * LANES, :], 0.0)
        qtb_ref[h] = (qt * LOG2_E).astype(BF16)

        gate = jnp.dot(kmean_ref[:, lanes], qt, precision=HIGHEST, preferred_element_type=F32)
        gate = jnp.where(blk_id < own, gate, -jnp.inf)
        sel = jnp.zeros((nblk, MOBA_BLOCK), F32)
        for _ in range(MOBA_TOPK):
            best = jnp.max(gate, axis=0, keepdims=True)
            first = jnp.min(jnp.where(gate == best, blk_id, nblk), axis=0, keepdims=True)
            pick = (blk_id == first) & (best > -jnp.inf)
            sel = jnp.where(pick, 1.0, sel)
            gate = jnp.where(pick, -jnp.inf, gate)
        sel_ref[h] = sel
        m_ref[h] = jnp.full((1, MOBA_BLOCK), NEG_BIG, F32)
        acc_ref[h] = jnp.zeros((MOBA_PV_ROWS, MOBA_BLOCK), F32)

    def scores(j, h, slot):
        lanes = pl.ds((h // 2) * LANES, LANES)
        s_ref[slot * MOBA_HEADS + h] = jnp.dot(k_ref[j, :, lanes], qtb_ref[h],
                                               preferred_element_type=F32).astype(BF16)

    def attend(j, h, slot, selected, causal):
        s = s_ref[slot * MOBA_HEADS + h]
        if causal:
            s = jnp.where(key_pos <= q_pos, s, NEG_BIG)
        m_run = m_ref[h]
        parts = s.reshape(MOBA_MAX_PARTS, MOBA_BLOCK // MOBA_MAX_PARTS, MOBA_BLOCK)
        blk_max = jnp.max(jnp.max(parts, axis=0), axis=0, keepdims=True)
        m_new = jnp.maximum(m_run, jnp.where(selected, blk_max.astype(F32), NEG_BIG))
        shift = jnp.where(selected, m_new, -NEG_BIG).astype(BF16)
        p = jnp.exp2(s - shift)
        v_aug = jnp.concatenate([vt_ref[j, pl.ds(h * MOBA_DH, MOBA_DH), :], ones_rows], axis=0)
        pv = jnp.dot(v_aug, p, preferred_element_type=F32)
        acc_ref[h] = acc_ref[h] * jnp.exp2(m_run - m_new) + pv
        m_ref[h] = m_new

    def past(j, slot):
        for pair in range(MOBA_HEADS // 2):
            scores(j + 1, 2 * pair, 1 - slot)
            scores(j + 1, 2 * pair + 1, 1 - slot)
            for h in (2 * pair, 2 * pair + 1):
                attend(j, h, slot, sel_ref[h, pl.ds(j, 1), :] > 0.0, False)

    def past_group(t, carry):
        for u in range(MOBA_UNROLL):
            past(MOBA_UNROLL * t + u, u % 2)
        return carry

    def own_block(slot):
        everyone = jnp.full((1, MOBA_BLOCK), True)
        for h in range(MOBA_HEADS):
            attend(own, h, slot, everyone, True)

    for h in range(MOBA_HEADS):
        scores(0, h, 0)
    lax.fori_loop(0, own // MOBA_UNROLL, past_group, 0)
    done = (own // MOBA_UNROLL) * MOBA_UNROLL
    for u in range(MOBA_UNROLL - 1):

        @pl.when(own - done > u)
        def _():
            past(done + u, u % 2)

    @pl.when(own % 2 == 0)
    def _():
        own_block(0)

    @pl.when(own % 2 == 1)
    def _():
        own_block(1)

    for h in range(MOBA_HEADS):
        out_ref[h * MOBA_DH:(h + 1) * MOBA_DH, :] = acc_ref[h, 0:MOBA_DH, :] / acc_ref[h, MOBA_DH:MOBA_DH + 1, :]
    o_ref[...] = out_ref[...].T


def _moba(qt3, k3, vt3, kmean):
    nblk = k3.shape[0]
    seq = nblk * MOBA_BLOCK
    return pl.pallas_call(
        _moba_kernel,
        grid=(nblk,),
        in_specs=[pl.BlockSpec((1, MOBA_W, MOBA_BLOCK), lambda i: (i, 0, 0)),
                  _const_spec((nblk, MOBA_BLOCK, MOBA_W)), _const_spec((nblk, MOBA_W, MOBA_BLOCK)),
                  _const_spec((nblk, MOBA_W))],
        out_specs=pl.BlockSpec((MOBA_BLOCK, MOBA_W), lambda i: (i, 0)),
        out_shape=jax.ShapeDtypeStruct((seq, MOBA_W), F32),
        scratch_shapes=[pltpu.VMEM((MOBA_HEADS, nblk, MOBA_BLOCK), F32),
                        pltpu.VMEM((MOBA_HEADS, LANES, MOBA_BLOCK), BF16),
                        pltpu.VMEM((MOBA_HEADS, 1, MOBA_BLOCK), F32),
                        pltpu.VMEM((MOBA_HEADS, MOBA_PV_ROWS, MOBA_BLOCK), F32),
                        pltpu.VMEM((MOBA_W, MOBA_BLOCK), F32),
                        pltpu.VMEM((2 * MOBA_HEADS, MOBA_BLOCK, MOBA_BLOCK), BF16)],
        compiler_params=_params("parallel"),
        name="moba",
    )(qt3, k3, vt3, kmean)


def _merge_kernel(x_ref, g_ref, gla_ref, moba_ref, loc_ref, halo_ref, wpool_ref, pscale_ref, gsgu_ref, wsgu_ref,
                  bsgu_ref, wgate_ref, wbr_ref, wout_ref, y_ref, zpad_ref):
    x = x_ref[...]
    h = _rms(x, g_ref[...]).astype(BF16)

    def gated(b, branch):
        gate = _sigmoid(jnp.dot(h, wgate_ref[:, b * D_MODEL:(b + 1) * D_MODEL], preferred_element_type=F32))
        return gate * jnp.dot(branch.astype(BF16), wbr_ref[b], preferred_element_type=F32)

    merged = gated(0, gla_ref[...]) + gated(3, moba_ref[...])
    pool, sgu = _local_mixers(loc_ref, halo_ref, wpool_ref, pscale_ref, gsgu_ref, wsgu_ref, bsgu_ref, zpad_ref)
    merged = merged + gated(1, pool) + gated(2, sgu)
    y_ref[...] = x + jnp.dot(merged.astype(BF16), wout_ref[...], preferred_element_type=F32)


def _merge(x, g, o_gla, o_moba, loc, wpool_bd, pscale, gsgu, wsgu, bsgu_full, wgate, wbr, wout):
    seq = x.shape[0]
    per = ROW_TILE // HALO
    row = lambda w: pl.BlockSpec((ROW_TILE, w), lambda i: (i, 0))
    return pl.pallas_call(
        _merge_kernel,
        grid=(seq // ROW_TILE,),
        in_specs=[row(D_MODEL), _const_spec((1, D_MODEL)), row(BRANCH_W), row(BRANCH_W), row(LOC_W),
                  pl.BlockSpec((HALO, BRANCH_W), lambda i: (jnp.maximum(i * per - 1, 0), 0)),
                  _const_spec((BRANCH_W, BRANCH_W)), _const_spec((1, BRANCH_W)), _const_spec((1, BRANCH_W)),
                  _const_spec((SGU_GROUPS, SGU_CHUNK, SGU_CHUNK)), _const_spec((SGU_CHUNK, BRANCH_W)),
                  _const_spec((D_MODEL, N_BRANCH * D_MODEL)), _const_spec((N_BRANCH, BRANCH_W, D_MODEL)),
                  _const_spec((D_MODEL, D_MODEL))],
        out_specs=row(D_MODEL),
        out_shape=jax.ShapeDtypeStruct((seq, D_MODEL), F32),
        scratch_shapes=[pltpu.VMEM((HALO + ROW_TILE, BRANCH_W), F32)],
        compiler_params=_params("parallel"),
        name="merge",
    )(x, g, o_gla, o_moba, loc, loc, wpool_bd, pscale, gsgu, wsgu, bsgu_full, wgate, wbr, wout)


def _ffn_kernel(x_ref, halo_ref, g_ref, wup_ref, cw_ref, cb_ref, wdown_ref, gfin_ref, y_ref,
                h_ref, acc_ref, up_ref, act_ref, *, final_norm):
    rows = x_ref.shape[0]
    x = x_ref[...]
    h_halo = _rms(halo_ref[...], g_ref[...])
    h_ref[0:HALO, :] = jnp.where(pl.program_id(0) > 0, h_halo, 0.0).astype(BF16)
    h_ref[HALO:HALO + rows, :] = _rms(x, g_ref[...]).astype(BF16)
    n_chunks = D_FF // FF_CHUNK

    def up_proj(c, slot):
        for half in range(2):
            cols = pl.ds(half * D_FF + c * FF_CHUNK, FF_CHUNK)
            up_ref[slot, half] = jnp.dot(h_ref[...], wup_ref[:, cols], preferred_element_type=F32)

    def conv(c, slot, half):
        cols = pl.ds(half * D_FF + c * FF_CHUNK, FF_CHUNK)
        cw = cw_ref[:, cols]
        return (up_ref[slot, half, HALO - 2:HALO - 2 + rows, :] * cw[0:1] +
                up_ref[slot, half, HALO - 1:HALO - 1 + rows, :] * cw[1:2] +
                up_ref[slot, half, HALO:HALO + rows, :] * cw[2:3] + cb_ref[:, cols])

    acc_ref[...] = x
    up_proj(0, 0)
    for c in range(n_chunks + 1):
        if c + 1 < n_chunks:
            up_proj(c + 1, (c + 1) % 2)
        if c >= 1:
            acc_ref[...] += jnp.dot(act_ref[(c - 1) % 2], wdown_ref[(c - 1) * FF_CHUNK:c * FF_CHUNK, :],
                                    preferred_element_type=F32)
        if c < n_chunks:
            a = conv(c, c % 2, 0)
            b = conv(c, c % 2, 1)
            act_ref[c % 2] = (a * _sigmoid(a) * b).astype(BF16)
    y = acc_ref[...]
    y_ref[...] = _rms(y, gfin_ref[...]) if final_norm else y


def _ffn(x, g, wup, cw, cb, wdown, gfin, final_norm):
    seq = x.shape[0]
    per = FFN_TILE // HALO
    row = pl.BlockSpec((FFN_TILE, D_MODEL), lambda i: (i, 0))
    return pl.pallas_call(
        functools.partial(_ffn_kernel, final_norm=final_norm),
        grid=(seq // FFN_TILE,),
        in_specs=[row, pl.BlockSpec((HALO, D_MODEL), lambda i: (jnp.maximum(i * per - 1, 0), 0)),
                  _const_spec((1, D_MODEL)), _const_spec((D_MODEL, 2 * D_FF)), _const_spec((3, 2 * D_FF)),
                  _const_spec((1, 2 * D_FF)), _const_spec((D_FF, D_MODEL)), _const_spec((1, D_MODEL))],
        out_specs=row,
        out_shape=jax.ShapeDtypeStruct((seq, D_MODEL), F32),
        scratch_shapes=[pltpu.VMEM((HALO + FFN_TILE, D_MODEL), BF16), pltpu.VMEM((FFN_TILE, D_MODEL), F32),
                        pltpu.VMEM((2, 2, HALO + FFN_TILE, FF_CHUNK), F32),
                        pltpu.VMEM((2, FFN_TILE, FF_CHUNK), BF16)],
        compiler_params=_params("parallel"),
        name="conv_ffn",
    )(x, x, g, wup, cw, cb, wdown, gfin)


def _pack_w_in_kernel(wt_ref, o_ref):
    n_gate_end = C_GA + GLA_RANK
    d_model = wt_ref.shape[2]
    for c in range(C_END // LANES):
        if (c + 1) * LANES <= C_GA:
            cols = wt_ref[0, c * LANES:(c + 1) * LANES, :]
        elif c * LANES == C_GA:
            cols = jnp.concatenate([wt_ref[0, C_GA:n_gate_end, :],
                                    jnp.zeros((LANES - GLA_RANK, d_model), F32)], axis=0)
        else:
            src = n_gate_end + c * LANES - C_LOC
            cols = wt_ref[0, src:src + LANES, :]
        o_ref[:, c * LANES:(c + 1) * LANES] = cols.T.astype(BF16)


def _pack_w_in(w_in, layer):
    _, rows, cols = w_in.shape
    return pl.pallas_call(
        _pack_w_in_kernel,
        grid=(1,),
        in_specs=[pl.BlockSpec((1, cols, rows), lambda i: (layer, 0, 0))],
        out_specs=pl.BlockSpec((rows, C_END), lambda i: (0, 0)),
        out_shape=jax.ShapeDtypeStruct((rows, C_END), BF16),
        compiler_params=_params("arbitrary"),
        name="pack_w_in",
    )(jnp.swapaxes(w_in, 1, 2))


def _cast_kernel(w_ref, o_ref):
    o_ref[...] = w_ref[0].astype(o_ref.dtype)


def _layer_bf16(w, layer):
    _, rows, cols = w.shape
    block = min(rows, CAST_ROWS)
    return pl.pallas_call(
        _cast_kernel,
        grid=(rows // block,),
        in_specs=[pl.BlockSpec((1, block, cols), lambda i: (layer, i, 0))],
        out_specs=pl.BlockSpec((block, cols), lambda i: (i, 0)),
        out_shape=jax.ShapeDtypeStruct((rows, cols), BF16),
        compiler_params=_params("parallel"),
        name="cast_bf16",
    )(w)


def _block_diag(w):
    g, a, b = w.shape
    eye = jnp.eye(g, dtype=w.dtype)
    return (eye[:, None, :, None] * w[:, :, None, :]).reshape(g * a, g * b)


def kernel(x, positions, g_mix, w_in, w_gla_a, b_gla_a, g_gla_norm, w_pool, pool_scale, g_sgu, w_sgu, b_sgu,
           w_gate, w_branch, w_out, g_ffn, w_up, conv_w, conv_b, w_down, g_final):
    bsz, seq, d = x.shape
    assert bsz == 1 and d == D_MODEL and seq % FFN_TILE == 0 and seq % ROW_TILE == 0 and seq % GLA_TILE == 0
    depth = w_in.shape[0]
    xs = x[0]
    pos = positions.reshape(seq, 1)
    half = MOBA_DH // 2
    inv_freq = ROPE_THETA ** (-jnp.arange(half, dtype=F32) / half)
    invf = jnp.tile(inv_freq, LANES // half)[None, :]
    row = lambda a: a.reshape(1, -1)

    for l in range(depth):
        wa2 = jnp.concatenate([w_gla_a[l], jnp.zeros((LANES - GLA_RANK, GLA_QK), F32)], axis=0)
        gla_in, loc, qt3, k, vt3, kmean = _inproj(xs, pos, row(g_mix[l]), _pack_w_in(w_in, l), wa2,
                                                  row(b_gla_a[l]), invf)
        o_gla = _gla(gla_in, row(jnp.tile(g_gla_norm[l], GLA_HEADS)))
        o_moba = _moba(qt3, k.reshape(seq // MOBA_BLOCK, MOBA_BLOCK, MOBA_W), vt3,
                       kmean.reshape(seq // MOBA_BLOCK, MOBA_W))
        xs = _merge(xs, row(g_mix[l]), o_gla, o_moba, loc, _block_diag(w_pool[l]).astype(BF16),
                    row(pool_scale[l]), row(g_sgu[l]), w_sgu[l], jnp.repeat(b_sgu[l].T, SGU_GW, axis=1),
                    _layer_bf16(w_gate, l),
                    _layer_bf16(w_branch.reshape(depth, N_BRANCH * BRANCH_W, D_MODEL), l).reshape(
                        N_BRANCH, BRANCH_W, D_MODEL),
                    _layer_bf16(w_out, l))
        xs = _ffn(xs, row(g_ffn[l]), _layer_bf16(w_up, l), conv_w[l], row(conv_b[l]), _layer_bf16(w_down, l),
                  row(g_final), final_norm=(l == depth - 1))
    return xs[None]
```
